```python
import jax, jax.numpy as jnp
from jax import lax
import numpy as np

D_MODEL = 1024
BATCH = 32
SEQ = 2048
DEPTH = 1

PLE_DIM = 256
EPS = 1e-6
GMLP_WIDTH = 1024
GMLP_GROUPS = 4
GMLP_GROUP_DIM = GMLP_WIDTH // GMLP_GROUPS
GMLP_CHUNK = 128
HGRN_HEADS = 8
HGRN_KEY_DIM = 128
HGRN_VAL_DIM = 128
HGRN_FORGET_WIDTH = HGRN_HEADS * HGRN_KEY_DIM
HGRN_WIDTH = HGRN_HEADS * HGRN_VAL_DIM
HGRN_CHUNK = 32
COL_SIZES = (GMLP_WIDTH, GMLP_WIDTH, GMLP_WIDTH, HGRN_FORGET_WIDTH, HGRN_FORGET_WIDTH, HGRN_WIDTH, HGRN_WIDTH, D_MODEL, D_MODEL)
IN_COLS = sum(COL_SIZES)
SPLIT_POINTS = tuple(int(c) for c in np.cumsum(COL_SIZES)[:-1])

kernel_name = "hybrid_gmlp_hgrn2_gated_block"


def rms_norm(x, g):
    xf = x.astype(jnp.float32)
    y = xf * lax.rsqrt(jnp.mean(xf * xf, axis=-1, keepdims=True) + EPS)
    return (y * g.astype(jnp.float32)).astype(x.dtype)


def layer_norm(x, g, b):
    xf = x.astype(jnp.float32)
    mu = jnp.mean(xf, axis=-1, keepdims=True)
    xc = xf - mu
    y = xc * lax.rsqrt(jnp.mean(xc * xc, axis=-1, keepdims=True) + EPS)
    return (y * g.astype(jnp.float32) + b.astype(jnp.float32)).astype(x.dtype)


def gmlp_spatial_gating(u, v, ln_g, ln_b, w_s, b_s):
    bsz, t, _ = u.shape
    n = t // GMLP_CHUNK
    v = layer_norm(v, ln_g, ln_b)
    v = v.reshape(bsz, n, GMLP_CHUNK, GMLP_GROUPS, GMLP_GROUP_DIM)
    causal = jnp.tril(jnp.ones((GMLP_CHUNK, GMLP_CHUNK), dtype=bool))
    w = jnp.where(causal[None], w_s, jnp.zeros_like(w_s))
    v = jnp.einsum('gts,bnsgc->bntgc', w, v) + b_s.T[:, :, None]
    return u * v.reshape(bsz, t, GMLP_WIDTH)


def hgrn2_chunked(q, k, g, v):
    bsz, t = q.shape[:2]
    n = t // HGRN_CHUNK

    def to_chunks(a):
        return a.reshape(bsz, n, HGRN_CHUNK, HGRN_HEADS, a.shape[-1]).transpose(1, 0, 3, 2, 4)

    qc, kc, gc, vc = to_chunks(q), to_chunks(k), to_chunks(g), to_chunks(v)
    causal = jnp.tril(jnp.ones((HGRN_CHUNK, HGRN_CHUNK), dtype=bool))[:, :, None]

    def step(state, inp):
        qb, kb, gb, vb = inp
        cum = jnp.cumsum(gb, axis=2)
        o_inter = jnp.einsum('bhtd,bhdv->bhtv', qb * jnp.exp(cum), state)
        diff = cum[:, :, :, None, :] - cum[:, :, None, :, :]
        decay = jnp.exp(jnp.where(causal, diff, -jnp.inf))
        scores = jnp.einsum('bhtd,bhsd,bhtsd->bhts', qb, kb, decay)
        o = o_inter + jnp.einsum('bhts,bhsv->bhtv', scores, vb)
        last = cum[:, :, -1:, :]
        state = jnp.exp(last[:, :, 0, :])[..., None] * state + jnp.einsum('bhsd,bhsv->bhdv', kb * jnp.exp(last - cum), vb)
        return state, o

    s0 = jnp.zeros((bsz, HGRN_HEADS, HGRN_KEY_DIM, HGRN_VAL_DIM), jnp.float32)
    _, o = lax.scan(step, s0, (qc, kc, gc, vc))
    return o.transpose(1, 0, 3, 2, 4).reshape(bsz, t, HGRN_HEADS, HGRN_VAL_DIM)


def setup_inputs(seed: int = 0) -> dict:
    key = jax.random.key(seed)
    ks = jax.random.split(key, 20)
    f32 = jnp.float32

    def nrm(k, shape, scale):
        return jax.random.normal(k, shape, f32) * scale

    return {
        "x": nrm(ks[0], (BATCH, SEQ, D_MODEL), 1.0),
        "p": nrm(ks[1], (DEPTH, BATCH, SEQ, PLE_DIM), 1.0),
        "w_in": nrm(ks[2], (DEPTH, D_MODEL, IN_COLS), D_MODEL ** -0.5),
        "gmlp_ln_g": 1.0 + nrm(ks[3], (DEPTH, GMLP_WIDTH), 0.05),
        "gmlp_ln_b": nrm(ks[4], (DEPTH, GMLP_WIDTH), 0.02),
        "gmlp_w_s": nrm(ks[5], (DEPTH, GMLP_GROUPS, GMLP_CHUNK, GMLP_CHUNK), 0.5 * GMLP_CHUNK ** -0.5),
        "gmlp_b_s": 1.0 + nrm(ks[6], (DEPTH, GMLP_GROUPS, GMLP_CHUNK), 0.05),
        "hgrn_lb_logits": nrm(ks[7], (DEPTH + 1, HGRN_FORGET_WIDTH), 0.5),
        "hgrn_norm_g": 1.0 + nrm(ks[8], (DEPTH, HGRN_HEADS, HGRN_VAL_DIM), 0.05),
        "w_branch_a": nrm(ks[9], (DEPTH, GMLP_WIDTH, D_MODEL), GMLP_WIDTH ** -0.5),
        "w_branch_b": nrm(ks[10], (DEPTH, HGRN_WIDTH, D_MODEL), HGRN_WIDTH ** -0.5),
        "w_out": nrm(ks[11], (DEPTH, D_MODEL, D_MODEL), D_MODEL ** -0.5),
        "g_pre": 1.0 + nrm(ks[12], (DEPTH, D_MODEL), 0.05),
        "g_post": 1.0 + nrm(ks[13], (DEPTH, D_MODEL), 0.05),
        "w_ple": nrm(ks[14], (DEPTH, PLE_DIM, D_MODEL), PLE_DIM ** -0.5),
        "w_ple_gate": nrm(ks[15], (DEPTH, D_MODEL, D_MODEL), D_MODEL ** -0.5),
        "b_ple_gate": nrm(ks[16], (DEPTH, D_MODEL), 0.02),
        "g_ple": 1.0 + nrm(ks[17], (DEPTH, D_MODEL), 0.05),
    }


def reference(x, p, w_in, gmlp_ln_g, gmlp_ln_b, gmlp_w_s, gmlp_b_s, hgrn_lb_logits, hgrn_norm_g,
              w_branch_a, w_branch_b, w_out, g_pre, g_post, w_ple, w_ple_gate, b_ple_gate, g_ple):
    bsz, t, _ = x.shape
    lower_bounds = jnp.cumsum(jax.nn.softmax(hgrn_lb_logits.astype(jnp.float32), axis=0), axis=0)
    for i in range(DEPTH):
        h = rms_norm(x, g_pre[i])
        proj = h @ w_in[i]
        u, v, z_a, q, f, inp, z_b, a_a, a_b = jnp.split(proj, SPLIT_POINTS, axis=-1)

        y_a = gmlp_spatial_gating(jax.nn.gelu(u, approximate=False), jax.nn.gelu(v, approximate=False),
                                  gmlp_ln_g[i], gmlp_ln_b[i], gmlp_w_s[i], gmlp_b_s[i])
        y_a = (y_a * jax.nn.silu(z_a)) @ w_branch_a[i]

        lb = lower_bounds[i]
        forget = lb + (1.0 - lb) * jax.nn.sigmoid(f.astype(jnp.float32))
        k_in = (1.0 - forget).reshape(bsz, t, HGRN_HEADS, HGRN_KEY_DIM)
        log_f = jnp.log(forget).reshape(bsz, t, HGRN_HEADS, HGRN_KEY_DIM)
        q_h = (jax.nn.silu(q.astype(jnp.float32)) * (HGRN_KEY_DIM ** -0.5)).reshape(bsz, t, HGRN_HEADS, HGRN_KEY_DIM)
        v_h = inp.astype(jnp.float32).reshape(bsz, t, HGRN_HEADS, HGRN_VAL_DIM)
        o = hgrn2_chunked(q_h, k_in, log_f, v_h)
        o = rms_norm(o, hgrn_norm_g[i]).reshape(bsz, t, HGRN_WIDTH).astype(x.dtype)
        y_b = (o * jax.nn.silu(z_b)) @ w_branch_b[i]

        merged = jax.nn.sigmoid(a_a) * y_a + jax.nn.sigmoid(a_b) * y_b
        x = x + rms_norm(merged @ w_out[i], g_post[i])

        e = p[i].astype(x.dtype) @ w_ple[i]
        gate = jax.nn.sigmoid(x @ w_ple_gate[i] + b_ple_gate[i])
        x = x + rms_norm(e * gate, g_ple[i])
    return x
```

```python
import functools

import jax
import jax.numpy as jnp
from jax import lax
from jax.experimental import pallas as pl
from jax.experimental.pallas import tpu as pltpu

F32 = jnp.float32
BF16 = jnp.bfloat16

D_MODEL = 1024
PLE_DIM = 256
EPS = 1e-6
GMLP_GROUPS = 4
GMLP_GROUP_DIM = 256
GMLP_CHUNK = 128
HEADS = 8
HEAD_DIM = 128
HGRN_CHUNK = 32
LEVEL_HALVES = (16, 8, 4, 2, 1)
N_PROJ = 9

TM = 256
VMEM_LIMIT_BYTES = 60000 * 1024


def _rms(y, g):
    return y * lax.rsqrt(jnp.mean(y * y, axis=-1, keepdims=True) + EPS) * g


def _gelu(y):
    return 0.5 * y * (1.0 + lax.erf(y * (2.0 ** -0.5)))


def _sigmoid(y):
    return jax.nn.sigmoid(y)


def _silu(y):
    return y * _sigmoid(y)


def _neg_abs(y):
    u = lax.bitcast_convert_type(y, jnp.uint32) | jnp.uint32(0x80000000)
    return lax.bitcast_convert_type(u, F32)


def _bcast_row(a3, r):
    return jnp.broadcast_to(a3[:, r:r + 1, :], a3.shape)


def _body(x_ref, p_ref, win_ref, lng_ref, lnb_ref, wsp_ref, bs_ref, lbl_ref, hng_ref,
          wba_ref, wbb_ref, wout_ref, gpre_ref, gpost_ref, wple_ref, wpg_ref, bpg_ref, gple_ref,
          o_ref, st_ref):
    nc = TM // HGRN_CHUNK
    c_rows = HGRN_CHUNK

    @pl.when(pl.program_id(1) == 0)
    def _():
        st_ref[...] = jnp.zeros_like(st_ref)

    x = x_ref[...]
    h = _rms(x, gpre_ref[...]).astype(BF16)

    def proj(j):
        return jnp.dot(h, win_ref[:, j * D_MODEL:(j + 1) * D_MODEL], preferred_element_type=F32)

    gu = _gelu(proj(0))
    gv = _gelu(proj(1))
    mu = jnp.mean(gv, axis=-1, keepdims=True)
    vc = gv - mu
    vn = (vc * lax.rsqrt(jnp.mean(vc * vc, axis=-1, keepdims=True) + EPS) * lng_ref[...] + lnb_ref[...]).astype(BF16)
    r128 = lax.broadcasted_iota(jnp.int32, (GMLP_CHUNK, GMLP_CHUNK), 0)
    c128 = lax.broadcasted_iota(jnp.int32, (GMLP_CHUNK, GMLP_CHUNK), 1)
    wsp = [jnp.where(r128 >= c128, wsp_ref[g], jnp.zeros((), BF16)) for g in range(GMLP_GROUPS)]
    bsb = [jnp.broadcast_to(bs_ref[:, g:g + 1], (GMLP_CHUNK, GMLP_GROUP_DIM)) for g in range(GMLP_GROUPS)]
    sp_rows = []
    for c in range(TM // GMLP_CHUNK):
        parts = []
        for g in range(GMLP_GROUPS):
            vg = vn[c * GMLP_CHUNK:(c + 1) * GMLP_CHUNK, g * GMLP_GROUP_DIM:(g + 1) * GMLP_GROUP_DIM]
            parts.append(jnp.dot(wsp[g], vg, preferred_element_type=F32) + bsb[g])
        sp_rows.append(jnp.concatenate(parts, axis=1))
    sp = jnp.concatenate(sp_rows, axis=0)
    ya_in = (gu * sp * _silu(proj(2))).astype(BF16)
    y_a = jnp.dot(ya_in, wba_ref[...], preferred_element_type=F32)

    lbl = lbl_ref[...]
    le = jnp.exp(lbl - jnp.max(lbl, axis=0, keepdims=True))
    lb = le[0:1, :] / jnp.sum(le, axis=0, keepdims=True)
    qh = _silu(proj(3)) * (HEAD_DIM ** -0.5)
    forget = lb + (1.0 - lb) * _sigmoid(proj(4))
    kk = 1.0 - forget
    lf = jnp.log(forget)
    vv = proj(5).astype(BF16)

    rt = lax.broadcasted_iota(jnp.int32, (TM, TM), 0)
    ct = lax.broadcasted_iota(jnp.int32, (TM, TM), 1)
    tri = jnp.where((rt >= ct) & ((rt // c_rows) == (ct // c_rows)), 1.0, 0.0).astype(BF16)
    lf_hi = lf.astype(BF16)
    lf_lo = (lf - lf_hi.astype(F32)).astype(BF16)
    b = jnp.dot(tri, lf_hi, preferred_element_type=F32) + jnp.dot(tri, lf_lo, preferred_element_type=F32)

    shp3 = (nc, c_rows, D_MODEL)
    b3 = b.reshape(shp3)
    q3 = qh.reshape(shp3)
    k3 = kk.reshape(shp3)
    v3 = vv.reshape(shp3)
    row = lax.broadcasted_iota(jnp.int32, (1, c_rows, D_MODEL), 1)

    xs = []
    for hs in LEVEL_HALVES:
        if hs == 1:
            d = jnp.where((row % 2) == 1, lf.reshape(shp3), 0.0)
        else:
            if hs >= 8:
                bg = b.reshape(TM // (2 * hs), 2 * hs, D_MODEL)
                r = _bcast_row(bg, hs - 1).reshape(shp3)
            else:
                b8 = b.reshape(TM // 8, 8, D_MODEL)
                if hs == 4:
                    r = _bcast_row(b8, 3)
                else:
                    row8 = lax.broadcasted_iota(jnp.int32, (1, 8, D_MODEL), 1)
                    r = jnp.where(row8 < 4, _bcast_row(b8, 1), _bcast_row(b8, 5))
                r = r.reshape(shp3)
            d = _neg_abs(b3 - r)
        is_q = (row % (2 * hs)) >= hs
        xs.append((jnp.where(is_q, q3, k3) * jnp.exp(d)).astype(BF16))
    q3b = q3.astype(BF16)
    k3b = k3.astype(BF16)
    blast = _bcast_row(b3, c_rows - 1)
    qs = (q3 * jnp.exp(b3)).astype(BF16)
    ks = (k3 * jnp.exp(blast - b3)).astype(BF16)
    dl = jnp.exp(b3[:, c_rows - 1:c_rows, :])

    ti = lax.broadcasted_iota(jnp.int32, (1, c_rows, c_rows), 1)
    si = lax.broadcasted_iota(jnp.int32, (1, c_rows, c_rows), 2)
    lvl_masks = [((ti // (2 * hs)) == (si // (2 * hs))) & ((ti % (2 * hs)) >= hs) & ((si % (2 * hs)) < hs)
                 for hs in LEVEL_HALVES]
    diag_mask = ti == si

    hng = hng_ref[...]
    o_heads = []
    for hd in range(HEADS):
        sl = slice(hd * HEAD_DIM, (hd + 1) * HEAD_DIM)
        sc = jnp.where(diag_mask, jnp.einsum('ctd,csd->cts', q3b[:, :, sl], k3b[:, :, sl],
                                             preferred_element_type=F32), 0.0)
        for xl, m in zip(xs, lvl_masks):
            xh = xl[:, :, sl]
            sc = sc + jnp.where(m, jnp.einsum('ctd,csd->cts', xh, xh, preferred_element_type=F32), 0.0)
        vh = v3[:, :, sl]
        o_h = jnp.einsum('cts,csv->ctv', sc.astype(BF16), vh, preferred_element_type=F32)
        upd = jnp.einsum('csv,csd->cvd', vh, ks[:, :, sl], preferred_element_type=F32)
        st = st_ref[hd]
        o_inter = []
        for c in range(nc):
            o_inter.append(lax.dot_general(qs[c, :, sl], st.astype(BF16), (((1,), (1,)), ((), ())),
                                           preferred_element_type=F32))
            st = st * dl[c][:, sl] + upd[c]
        st_ref[hd] = st
        o_h = o_h.reshape(TM, HEAD_DIM) + jnp.concatenate(o_inter, axis=0)
        o_heads.append(_rms(o_h, hng[:, sl]))
    o_all = jnp.concatenate(o_heads, axis=1)
    yb_in = (o_all * _silu(proj(6))).astype(BF16)
    y_b = jnp.dot(yb_in, wbb_ref[...], preferred_element_type=F32)

    merged = (_sigmoid(proj(7)) * y_a + _sigmoid(proj(8)) * y_b).astype(BF16)
    x1 = x + _rms(jnp.dot(merged, wout_ref[...], preferred_element_type=F32), gpost_ref[...])

    e = jnp.dot(p_ref[...].astype(BF16), wple_ref[...], preferred_element_type=F32)
    gate = _sigmoid(jnp.dot(x1.astype(BF16), wpg_ref[...], preferred_element_type=F32) + bpg_ref[...])
    o_ref[...] = x1 + _rms(e * gate, gple_ref[...])


def _const_spec(shape):
    nd = len(shape)
    return pl.BlockSpec(shape, lambda b, t: (0,) * nd, pipeline_mode=pl.Buffered(1))


@jax.jit
def kernel(x, p, w_in, gmlp_ln_g, gmlp_ln_b, gmlp_w_s, gmlp_b_s, hgrn_lb_logits, hgrn_norm_g, w_branch_a, w_branch_b, w_out, g_pre, g_post, w_ple, w_ple_gate, b_ple_gate, g_ple):
    bsz, t, d = x.shape
    assert d == D_MODEL and t % TM == 0 and p.shape[0] == 1
    row = lambda a: a.reshape(1, -1).astype(F32)
    consts = (
        w_in[0].astype(BF16),
        row(gmlp_ln_g[0]), row(gmlp_ln_b[0]),
        gmlp_w_s[0].astype(BF16),
        gmlp_b_s[0].T.astype(F32),
        hgrn_lb_logits.astype(F32),
        row(hgrn_norm_g[0]),
        w_branch_a[0].astype(BF16), w_branch_b[0].astype(BF16), w_out[0].astype(BF16),
        row(g_pre[0]), row(g_post[0]),
        w_ple[0].astype(BF16), w_ple_gate[0].astype(BF16),
        row(b_ple_gate[0]), row(g_ple[0]),
    )
    tile = lambda width: pl.BlockSpec((None, TM, width), lambda b, t: (b, t, 0))
    return pl.pallas_call(
        _body,
        out_shape=jax.ShapeDtypeStruct(x.shape, x.dtype),
        grid=(bsz, t // TM),
        in_specs=[tile(D_MODEL), tile(PLE_DIM)] + [_const_spec(c.shape) for c in consts],
        out_specs=tile(D_MODEL),
        scratch_shapes=[pltpu.VMEM((HEADS, HEAD_DIM, HEAD_DIM), F32)],
        compiler_params=pltpu.CompilerParams(
            dimension_semantics=("arbitrary", "arbitrary"),
            vmem_limit_bytes=VMEM_LIMIT_BYTES,
        ),
        name="hybrid_block",
    )(x, p[0], *consts)
```

```python
import jax
import jax.numpy as jnp
from jax import lax
from jax.experimental import pallas as pl
from jax.experimental.pallas import tpu as pltpu

F32 = jnp.float32
BF16 = jnp.bfloat16

D_MODEL = 1024
PLE_DIM = 256
EPS = 1e-6
GMLP_GROUPS = 4
GMLP_GROUP_DIM = 256
GMLP_CHUNK = 128
HEADS = 8
HEAD_DIM = 128
SUBLANES = 8

TM = 256
TOP_HALF = TM // 2
MXU_HALVES = (64, 32, 16, 8, 4, 2)
VMEM_LIMIT_BYTES = 60000 * 1024


def _rms(y, g):
    return y * lax.rsqrt(jnp.mean(y * y, axis=-1, keepdims=True) + EPS) * g


def _gelu(y):
    return 0.5 * y * (1.0 + lax.erf(y * (2.0 ** -0.5)))


def _sigmoid(y):
    return jax.nn.sigmoid(y)


def _silu(y):
    return y * _sigmoid(y)


def _neg_abs(y):
    u = lax.bitcast_convert_type(y, jnp.uint32) | jnp.uint32(0x80000000)
    return lax.bitcast_convert_type(u, F32)


def _dot_nt(a, b):
    return lax.dot_general(a, b, (((1,), (1,)), ((), ())), preferred_element_type=F32)


def _dot_tn(a, b):
    return lax.dot_general(a, b, (((0,), (0,)), ((), ())), preferred_element_type=F32)


def _body(x_ref, p_ref, win_ref, lng_ref, lnb_ref, wsp_ref, bs_ref, lbl_ref, hng_ref,
          wba_ref, wbb_ref, wout_ref, gpre_ref, gpost_ref, wple_ref, wpg_ref, bpg_ref, gple_ref,
          o_ref, st_ref):
    @pl.when(pl.program_id(1) == 0)
    def _():
        st_ref[...] = jnp.zeros_like(st_ref)

    x = x_ref[...]
    h = _rms(x, gpre_ref[...]).astype(BF16)

    def proj(j):
        return jnp.dot(h, win_ref[:, j * D_MODEL:(j + 1) * D_MODEL], preferred_element_type=F32)

    lbl = lbl_ref[...]
    le = jnp.exp(lbl - jnp.max(lbl, axis=0, keepdims=True))
    lb = le[0:1, :] / jnp.sum(le, axis=0, keepdims=True)
    forget = lb + (1.0 - lb) * _sigmoid(proj(4))
    kk = 1.0 - forget
    lf = jnp.log(forget)
    qh = _silu(proj(3)) * (HEAD_DIM ** -0.5)
    vf = proj(5)
    vv = vf.astype(BF16)

    gv = _gelu(proj(1))
    mu = jnp.mean(gv, axis=-1, keepdims=True)
    vc = gv - mu
    vn = (vc * lax.rsqrt(jnp.mean(vc * vc, axis=-1, keepdims=True) + EPS) * lng_ref[...] + lnb_ref[...]).astype(BF16)
    gu = _gelu(proj(0))
    sza = _silu(proj(2))
    szb = _silu(proj(6))
    ga = _sigmoid(proj(7))
    gb = _sigmoid(proj(8))
    e_ple = jnp.dot(p_ref[...].astype(BF16), wple_ref[...], preferred_element_type=F32)

    ti = lax.broadcasted_iota(jnp.int32, (TM, TM), 0)
    si = lax.broadcasted_iota(jnp.int32, (TM, TM), 1)
    tri = jnp.where(ti >= si, 1.0, 0.0).astype(BF16)
    lf_hi = lf.astype(BF16)
    lf_lo = (lf - lf_hi.astype(F32)).astype(BF16)
    b = jnp.dot(tri, lf_hi, preferred_element_type=F32) + jnp.dot(tri, lf_lo, preferred_element_type=F32)

    r128 = lax.broadcasted_iota(jnp.int32, (GMLP_CHUNK, GMLP_CHUNK), 0)
    c128 = lax.broadcasted_iota(jnp.int32, (GMLP_CHUNK, GMLP_CHUNK), 1)
    wsp = [jnp.where(r128 >= c128, wsp_ref[g], jnp.zeros((), BF16)) for g in range(GMLP_GROUPS)]
    bsb = [jnp.broadcast_to(bs_ref[:, g:g + 1], (GMLP_CHUNK, GMLP_GROUP_DIM)) for g in range(GMLP_GROUPS)]
    sp_rows = []
    for c in range(TM // GMLP_CHUNK):
        parts = []
        for g in range(GMLP_GROUPS):
            vg = vn[c * GMLP_CHUNK:(c + 1) * GMLP_CHUNK, g * GMLP_GROUP_DIM:(g + 1) * GMLP_GROUP_DIM]
            parts.append(jnp.dot(wsp[g], vg, preferred_element_type=F32) + bsb[g])
        sp_rows.append(jnp.concatenate(parts, axis=1))
    sp = jnp.concatenate(sp_rows, axis=0)
    ya_in = (gu * sp * sza).astype(BF16)
    y_a = jnp.dot(ya_in, wba_ref[...], preferred_element_type=F32)

    def level_operand(hs):
        if hs >= SUBLANES:
            shp = (TM // (2 * hs), 2 * hs, D_MODEL)
            b3, q3, k3 = b.reshape(shp), qh.reshape(shp), kk.reshape(shp)
            r = b3[:, hs - 1:hs, :]
            xk = k3[:, :hs, :] * jnp.exp(r - b3[:, :hs, :])
            xq = q3[:, hs:, :] * jnp.exp(b3[:, hs:, :] - r)
            return xk, xq
        shp = (TM // SUBLANES, SUBLANES, D_MODEL)
        b8, q8, k8 = b.reshape(shp), qh.reshape(shp), kk.reshape(shp)
        row8 = lax.broadcasted_iota(jnp.int32, (1, SUBLANES, D_MODEL), 1)
        if hs == 4:
            r = jnp.broadcast_to(b8[:, 3:4, :], shp)
        else:
            r = jnp.where(row8 < 4, jnp.broadcast_to(b8[:, 1:2, :], shp), jnp.broadcast_to(b8[:, 5:6, :], shp))
        x = jnp.where((row8 % (2 * hs)) >= hs, q8, k8) * jnp.exp(_neg_abs(b8 - r))
        return x.reshape(TM, D_MODEL).astype(BF16)

    xk_top, xq_top = level_operand(TOP_HALF)
    xk_top = xk_top.reshape(TOP_HALF, D_MODEL).astype(BF16)
    xq_top = xq_top.reshape(TOP_HALF, D_MODEL).astype(BF16)
    xs = []
    for hs in MXU_HALVES:
        if hs >= SUBLANES:
            xk, xq = level_operand(hs)
            xs.append(jnp.concatenate([xk, xq], axis=1).reshape(TM, D_MODEL).astype(BF16))
        else:
            xs.append(level_operand(hs))
    masks = [((ti ^ si) < 2 * hs) & ((ti & hs) != 0) & ((si & hs) == 0) for hs in MXU_HALVES]

    b_last = b[TM - 1:TM, :]
    qs = (qh * jnp.exp(b)).astype(BF16)
    ks = (kk * jnp.exp(b_last - b)).astype(BF16)
    dl = jnp.exp(b_last)

    rows = lax.broadcasted_iota(jnp.int32, (TM, D_MODEL), 0)
    w0 = qh * kk
    w1 = jnp.where((rows & 1) == 1, qh * pltpu.roll(kk, 1, 0) * jnp.exp(lf), 0.0)
    v_prev = pltpu.roll(vf, 1, 0)

    hng = hng_ref[...]
    o_heads = []
    for hd in range(HEADS):
        sl = slice(hd * HEAD_DIM, (hd + 1) * HEAD_DIM)
        a = jnp.zeros((TM, TM), F32)
        for xl, m in zip(reversed(xs), reversed(masks)):
            xh = xl[:, sl]
            a = jnp.where(m, _dot_nt(xh, xh), a)
        p_top = _dot_nt(xq_top[:, sl], xk_top[:, sl])
        a = jnp.concatenate([a[:TOP_HALF], jnp.concatenate([p_top, a[TOP_HALF:, TOP_HALF:]], axis=1)], axis=0)
        vh = vv[:, sl]
        st = st_ref[hd]
        o_h = jnp.dot(a.astype(BF16), vh, preferred_element_type=F32) + _dot_nt(qs[:, sl], st.astype(BF16))
        o_h = o_h + jnp.sum(w0[:, sl], axis=-1, keepdims=True) * vf[:, sl]
        o_h = o_h + jnp.sum(w1[:, sl], axis=-1, keepdims=True) * v_prev[:, sl]
        st_ref[hd] = st * dl[:, sl] + _dot_tn(vh, ks[:, sl])
        o_heads.append(_rms(o_h, hng[:, sl]))
    o_all = jnp.concatenate(o_heads, axis=1)
    yb_in = (o_all * szb).astype(BF16)
    y_b = jnp.dot(yb_in, wbb_ref[...], preferred_element_type=F32)

    merged = (ga * y_a + gb * y_b).astype(BF16)
    x1 = x + _rms(jnp.dot(merged, wout_ref[...], preferred_element_type=F32), gpost_ref[...])

    gate = _sigmoid(jnp.dot(x1.astype(BF16), wpg_ref[...], preferred_element_type=F32) + bpg_ref[...])
    o_ref[...] = x1 + _rms(e_ple * gate, gple_ref[...])


def _const_spec(shape):
    nd = len(shape)
    return pl.BlockSpec(shape, lambda b, t: (0,) * nd, pipeline_mode=pl.Buffered(1))


@jax.jit
def kernel(x, p, w_in, gmlp_ln_g, gmlp_ln_b, gmlp_w_s, gmlp_b_s, hgrn_lb_logits, hgrn_norm_g, w_branch_a, w_branch_b, w_out, g_pre, g_post, w_ple, w_ple_gate, b_ple_gate, g_ple):
    bsz, t, d = x.shape
    assert d == D_MODEL and t % TM == 0 and p.shape[0] == 1
    row = lambda a: a.reshape(1, -1).astype(F32)
    consts = (
        w_in[0].astype(BF16),
        row(gmlp_ln_g[0]), row(gmlp_ln_b[0]),
        gmlp_w_s[0].astype(BF16),
        gmlp_b_s[0].T.astype(F32),
        hgrn_lb_logits.astype(F32),
        row(hgrn_norm_g[0]),
        w_branch_a[0].astype(BF16), w_branch_b[0].astype(BF16), w_out[0].astype(BF16),
        row(g_pre[0]), row(g_post[0]),
        w_ple[0].astype(BF16), w_ple_gate[0].astype(BF16),
        row(b_ple_gate[0]), row(g_ple[0]),
    )
    tile = lambda width: pl.BlockSpec((None, TM, width), lambda b, t: (b, t, 0))
    return pl.pallas_call(
        _body,
        out_shape=jax.ShapeDtypeStruct(x.shape, x.dtype),
        grid=(bsz, t // TM),
        in_specs=[tile(D_MODEL), tile(PLE_DIM)] + [_const_spec(c.shape) for c in consts],
        out_specs=tile(D_MODEL),
        scratch_shapes=[pltpu.VMEM((HEADS, HEAD_DIM, HEAD_DIM), F32)],
        compiler_params=pltpu.CompilerParams(
            dimension_semantics=("arbitrary", "arbitrary"),
            vmem_limit_bytes=VMEM_LIMIT_BYTES,
        ),
        name="hybrid_block",
    )(x, p[0], *consts)
```

```python
import jax
import jax.numpy as jnp
from jax import lax
from jax.experimental import pallas as pl
from jax.experimental.pallas import tpu as pltpu

F32 = jnp.float32
BF16 = jnp.bfloat16

D_MODEL = 1024
PLE_DIM = 256
EPS = 1e-6
LOG2E = 1.4426950408889634
GMLP_GROUPS = 4
GMLP_GROUP_DIM = 256
GMLP_CHUNK = 128
HEADS = 8
HEAD_DIM = 128
SUBLANES = 8

TM = 256
TOP_HALF = TM // 2
MXU_HALVES = (64, 32, 16, 8, 4, 2)
VMEM_LIMIT_BYTES = 60000 * 1024


def _rms(y, g):
    return y * lax.rsqrt(jnp.mean(y * y, axis=-1, keepdims=True) + EPS) * g


def _gelu(y):
    return 0.5 * y * (1.0 + lax.erf(y * (2.0 ** -0.5)))


def _sigmoid(y):
    return jax.nn.sigmoid(y)


def _silu(y):
    return y * _sigmoid(y)


def _unpack(w_ref, c0=None, c1=None):
    w = w_ref[...] if c0 is None else w_ref[:, c0:c1]
    return pltpu.bitcast(w, BF16)


def _dot_nt(a, b):
    return lax.dot_general(a, b, (((1,), (1,)), ((), ())), preferred_element_type=F32)


def _dot_tn(a, b):
    return lax.dot_general(a, b, (((0,), (0,)), ((), ())), preferred_element_type=F32)


def _body(x_ref, p_ref, win_ref, lng_ref, lnb_ref, wsp_ref, bs_ref, lbl_ref, hng_ref,
          wba_ref, wbb_ref, wout_ref, gpre_ref, gpost_ref, wple_ref, wpg_ref, bpg_ref, gple_ref,
          o_ref, st_ref):
    @pl.when(pl.program_id(1) == 0)
    def _():
        st_ref[...] = jnp.zeros_like(st_ref)

    e_ple = jnp.dot(p_ref[...].astype(BF16), _unpack(wple_ref), preferred_element_type=F32)
    x = x_ref[...]
    h = _rms(x, gpre_ref[...]).astype(BF16)

    def proj(j):
        return jnp.dot(h, _unpack(win_ref, j * D_MODEL, (j + 1) * D_MODEL), preferred_element_type=F32)

    lbl = lbl_ref[...]
    le = jnp.exp(lbl - jnp.max(lbl, axis=0, keepdims=True))
    lb = le[0:1, :] / jnp.sum(le, axis=0, keepdims=True)
    forget = lb + (1.0 - lb) * _sigmoid(proj(4))
    kk = 1.0 - forget
    lf = jnp.log(forget)
    qh = _silu(proj(3)) * (HEAD_DIM ** -0.5)
    vf = proj(5)
    vv = vf.astype(BF16)

    gv = _gelu(proj(1))
    mu = jnp.mean(gv, axis=-1, keepdims=True)
    vc = gv - mu
    vn = (vc * lax.rsqrt(jnp.mean(vc * vc, axis=-1, keepdims=True) + EPS) * lng_ref[...] + lnb_ref[...]).astype(BF16)
    gu = _gelu(proj(0))
    sza = _silu(proj(2))
    szb = _silu(proj(6))

    ti = lax.broadcasted_iota(jnp.int32, (TM, TM), 0)
    si = lax.broadcasted_iota(jnp.int32, (TM, TM), 1)
    tri = jnp.where(ti >= si, 1.0, 0.0).astype(BF16)
    lf_hi = lf.astype(BF16)
    lf_lo = (lf - lf_hi.astype(F32)).astype(BF16)
    b = jnp.dot(tri, lf_hi, preferred_element_type=F32) + jnp.dot(tri, lf_lo, preferred_element_type=F32)
    b = b * LOG2E

    r128 = lax.broadcasted_iota(jnp.int32, (GMLP_CHUNK, GMLP_CHUNK), 0)
    c128 = lax.broadcasted_iota(jnp.int32, (GMLP_CHUNK, GMLP_CHUNK), 1)
    wsp = [jnp.where(r128 >= c128, wsp_ref[g], jnp.zeros((), BF16)) for g in range(GMLP_GROUPS)]
    bsb = [jnp.broadcast_to(bs_ref[:, g:g + 1], (GMLP_CHUNK, GMLP_GROUP_DIM)) for g in range(GMLP_GROUPS)]
    sp_rows = []
    for c in range(TM // GMLP_CHUNK):
        parts = []
        for g in range(GMLP_GROUPS):
            vg = vn[c * GMLP_CHUNK:(c + 1) * GMLP_CHUNK, g * GMLP_GROUP_DIM:(g + 1) * GMLP_GROUP_DIM]
            parts.append(jnp.dot(wsp[g], vg, preferred_element_type=F32) + bsb[g])
        sp_rows.append(jnp.concatenate(parts, axis=1))
    sp = jnp.concatenate(sp_rows, axis=0)
    ya_in = (gu * sp * sza).astype(BF16)
    y_a = jnp.dot(ya_in, _unpack(wba_ref), preferred_element_type=F32)

    def level_operand(hs):
        if hs >= SUBLANES:
            shp = (TM // (2 * hs), 2 * hs, D_MODEL)
            b3, q3, k3 = b.reshape(shp), qh.reshape(shp), kk.reshape(shp)
            r = b3[:, hs - 1:hs, :]
            xk = k3[:, :hs, :] * jnp.exp2(r - b3[:, :hs, :])
            xq = q3[:, hs:, :] * jnp.exp2(b3[:, hs:, :] - r)
            return xk, xq
        shp = (TM // SUBLANES, SUBLANES, D_MODEL)
        b8, q8, k8 = b.reshape(shp), qh.reshape(shp), kk.reshape(shp)
        row8 = lax.broadcasted_iota(jnp.int32, (1, SUBLANES, D_MODEL), 1)
        if hs == 4:
            r = jnp.broadcast_to(b8[:, 3:4, :], shp)
        else:
            r = jnp.where(row8 < 4, jnp.broadcast_to(b8[:, 1:2, :], shp), jnp.broadcast_to(b8[:, 5:6, :], shp))
        is_q = (row8 % (2 * hs)) >= hs
        d = (b8 - r) * jnp.where(is_q, 1.0, -1.0)
        x = jnp.where(is_q, q8, k8) * jnp.exp2(d)
        return x.reshape(TM, D_MODEL).astype(BF16)

    xk_top, xq_top = level_operand(TOP_HALF)
    xk_top = xk_top.reshape(TOP_HALF, D_MODEL).astype(BF16)
    xq_top = xq_top.reshape(TOP_HALF, D_MODEL).astype(BF16)
    xs = []
    for hs in MXU_HALVES:
        if hs >= SUBLANES:
            xk, xq = level_operand(hs)
            xs.append(jnp.concatenate([xk, xq], axis=1).reshape(TM, D_MODEL).astype(BF16))
        else:
            xs.append(level_operand(hs))
    masks = [((ti ^ si) < 2 * hs) & ((ti & hs) != 0) & ((si & hs) == 0) for hs in MXU_HALVES]

    b_last = b[TM - 1:TM, :]
    qs = (qh * jnp.exp2(b)).astype(BF16)
    ks = (kk * jnp.exp2(b_last - b)).astype(BF16)
    dl = jnp.exp2(b_last)

    rows = lax.broadcasted_iota(jnp.int32, (TM, D_MODEL), 0)
    w0 = qh * kk
    w1 = jnp.where((rows & 1) == 1, qh * pltpu.roll(kk, 1, 0) * jnp.exp(lf), 0.0)
    v_prev = pltpu.roll(vf, 1, 0)

    hng = hng_ref[...]
    o_heads = []

    def head_scores(hd):
        sl = slice(hd * HEAD_DIM, (hd + 1) * HEAD_DIM)
        a = jnp.zeros((TM, TM), F32)
        for xl, m in zip(reversed(xs), reversed(masks)):
            xh = xl[:, sl]
            a = jnp.where(m, _dot_nt(xh, xh), a)
        p_top = _dot_nt(xq_top[:, sl], xk_top[:, sl])
        a = jnp.concatenate([a[:TOP_HALF], jnp.concatenate([p_top, a[TOP_HALF:, TOP_HALF:]], axis=1)], axis=0)
        return a.astype(BF16)

    def head_output(hd, a):
        sl = slice(hd * HEAD_DIM, (hd + 1) * HEAD_DIM)
        vh = vv[:, sl]
        st = st_ref[hd]
        o_h = jnp.dot(a, vh, preferred_element_type=F32) + _dot_nt(qs[:, sl], st.astype(BF16))
        o_h = o_h + jnp.sum(w0[:, sl], axis=-1, keepdims=True) * vf[:, sl]
        o_h = o_h + jnp.sum(w1[:, sl], axis=-1, keepdims=True) * v_prev[:, sl]
        st_ref[hd] = st * dl[:, sl] + _dot_tn(vh, ks[:, sl])
        o_heads.append(_rms(o_h, hng[:, sl]))

    a_next = head_scores(0)
    for hd in range(HEADS):
        a_cur = a_next
        if hd + 1 < HEADS:
            a_next = head_scores(hd + 1)
        head_output(hd, a_cur)
    o_all = jnp.concatenate(o_heads, axis=1)
    yb_in = (o_all * szb).astype(BF16)

    halves = [slice(i * TOP_HALF, (i + 1) * TOP_HALF) for i in range(2)]
    y_b = [jnp.dot(yb_in[r], _unpack(wbb_ref), preferred_element_type=F32) for r in halves]
    ga = _sigmoid(proj(7))
    gb = _sigmoid(proj(8))
    mo = [jnp.dot((ga[r] * y_a[r] + gb[r] * y_b[i]).astype(BF16), _unpack(wout_ref), preferred_element_type=F32)
          for i, r in enumerate(halves)]
    x1 = [x[r] + _rms(mo[i], gpost_ref[...]) for i, r in enumerate(halves)]
    gl = [jnp.dot(x1[i].astype(BF16), _unpack(wpg_ref), preferred_element_type=F32) + bpg_ref[...] for i in range(2)]
    for i, r in enumerate(halves):
        o_ref[r, :] = x1[i] + _rms(e_ple[r] * _sigmoid(gl[i]), gple_ref[...])


def _const_spec(shape):
    nd = len(shape)
    return pl.BlockSpec(shape, lambda b, t: (0,) * nd, pipeline_mode=pl.Buffered(1))


@jax.jit
def kernel(x, p, w_in, gmlp_ln_g, gmlp_ln_b, gmlp_w_s, gmlp_b_s, hgrn_lb_logits, hgrn_norm_g, w_branch_a, w_branch_b, w_out, g_pre, g_post, w_ple, w_ple_gate, b_ple_gate, g_ple):
    bsz, t, d = x.shape
    assert d == D_MODEL and t % TM == 0 and p.shape[0] == 1
    row = lambda a: a.reshape(1, -1).astype(F32)

    def packed(w):
        k, n = w.shape
        wb = w.astype(BF16).reshape(k // 2, 2, n)
        return lax.bitcast_convert_type(jnp.swapaxes(wb, 1, 2), jnp.uint32)

    consts = (
        packed(w_in[0]),
        row(gmlp_ln_g[0]), row(gmlp_ln_b[0]),
        gmlp_w_s[0].astype(BF16),
        gmlp_b_s[0].T.astype(F32),
        hgrn_lb_logits.astype(F32),
        row(hgrn_norm_g[0]),
        packed(w_branch_a[0]), packed(w_branch_b[0]), packed(w_out[0]),
        row(g_pre[0]), row(g_post[0]),
        packed(w_ple[0]), packed(w_ple_gate[0]),
        row(b_ple_gate[0]), row(g_ple[0]),
    )
    tile = lambda width: pl.BlockSpec((None, TM, width), lambda b, t: (b, t, 0))
    return pl.pallas_call(
        _body,
        out_shape=jax.ShapeDtypeStruct(x.shape, x.dtype),
        grid=(bsz, t // TM),
        in_specs=[tile(D_MODEL), tile(PLE_DIM)] + [_const_spec(c.shape) for c in consts],
        out_specs=tile(D_MODEL),
        scratch_shapes=[pltpu.VMEM((HEADS, HEAD_DIM, HEAD_DIM), F32)],
        compiler_params=pltpu.CompilerParams(
            dimension_semantics=("arbitrary", "arbitrary"),
            vmem_limit_bytes=VMEM_LIMIT_BYTES,
        ),
        name="hybrid_block",
    )(x, p[0], *consts)
```

```python
import jax
import jax.numpy as jnp
from jax import lax
from jax.experimental import pallas as pl
from jax.experimental.pallas import tpu as pltpu

F32 = jnp.float32
BF16 = jnp.bfloat16

D_MODEL = 1024
PLE_DIM = 256
EPS = 1e-6
LOG2E = 1.4426950408889634
GMLP_GROUPS = 4
GMLP_GROUP_DIM = 256
GMLP_CHUNK = 128
HEADS = 8
HEAD_DIM = 128
SUBLANES = 8
LANES = 128

TM = 256
TOP_HALF = TM // 2
MXU_HALVES = (64, 32, 16, 8, 4, 2)
PACK_BLOCK_ROWS = 512
PACK_BLOCK_COLS = 1024
VMEM_LIMIT_BYTES = 60000 * 1024


def _rms(y, g):
    return y * lax.rsqrt(jnp.mean(y * y, axis=-1, keepdims=True) + EPS) * g


def _gelu(y):
    return 0.5 * y * (1.0 + lax.erf(y * (2.0 ** -0.5)))


def _sigmoid(y):
    return jax.nn.sigmoid(y)


def _silu(y):
    return y * _sigmoid(y)


def _unpack(w_ref, c0=None, c1=None):
    w = w_ref[...] if c0 is None else w_ref[:, c0:c1]
    return pltpu.bitcast(w, BF16)


def _dot_nt(a, b):
    return lax.dot_general(a, b, (((1,), (1,)), ((), ())), preferred_element_type=F32)


def _dot_tn(a, b):
    return lax.dot_general(a, b, (((0,), (0,)), ((), ())), preferred_element_type=F32)


def _body(x_ref, p_ref, win_ref, lng_ref, lnb_ref, wsp_ref, bs_ref, lbl_ref, hng_ref,
          wba_ref, wbb_ref, wout_ref, gpre_ref, gpost_ref, wple_ref, wpg_ref, bpg_ref, gple_ref,
          o_ref, st_ref):
    @pl.when(pl.program_id(1) == 0)
    def _():
        st_ref[...] = jnp.zeros_like(st_ref)

    e_ple = jnp.dot(p_ref[...].astype(BF16), _unpack(wple_ref), preferred_element_type=F32)
    x = x_ref[...]
    h = _rms(x, gpre_ref[...]).astype(BF16)

    def proj(j):
        return jnp.dot(h, _unpack(win_ref, j * D_MODEL, (j + 1) * D_MODEL), preferred_element_type=F32)

    lbl = lbl_ref[...]
    le = jnp.exp(lbl - jnp.max(lbl, axis=0, keepdims=True))
    lb = le[0:1, :] / jnp.sum(le, axis=0, keepdims=True)
    forget = lb + (1.0 - lb) * _sigmoid(proj(4))
    kk = 1.0 - forget
    lf = jnp.log(forget)
    qh = _silu(proj(3)) * (HEAD_DIM ** -0.5)
    vf = proj(5)
    vv = vf.astype(BF16)

    gv = _gelu(proj(1))
    mu = jnp.mean(gv, axis=-1, keepdims=True)
    vc = gv - mu
    vn = (vc * lax.rsqrt(jnp.mean(vc * vc, axis=-1, keepdims=True) + EPS) * lng_ref[...] + lnb_ref[...]).astype(BF16)
    gu = _gelu(proj(0))
    sza = _silu(proj(2))
    szb = _silu(proj(6))

    ti = lax.broadcasted_iota(jnp.int32, (TM, TM), 0)
    si = lax.broadcasted_iota(jnp.int32, (TM, TM), 1)
    tri = jnp.where(ti >= si, 1.0, 0.0).astype(BF16)
    lf_hi = lf.astype(BF16)
    lf_lo = (lf - lf_hi.astype(F32)).astype(BF16)
    b = jnp.dot(tri, lf_hi, preferred_element_type=F32) + jnp.dot(tri, lf_lo, preferred_element_type=F32)
    b = b * LOG2E

    r128 = lax.broadcasted_iota(jnp.int32, (GMLP_CHUNK, GMLP_CHUNK), 0)
    c128 = lax.broadcasted_iota(jnp.int32, (GMLP_CHUNK, GMLP_CHUNK), 1)
    wsp = [jnp.where(r128 >= c128, wsp_ref[g], jnp.zeros((), BF16)) for g in range(GMLP_GROUPS)]
    bsb = [jnp.broadcast_to(bs_ref[:, g:g + 1], (GMLP_CHUNK, GMLP_GROUP_DIM)) for g in range(GMLP_GROUPS)]
    sp_rows = []
    for c in range(TM // GMLP_CHUNK):
        parts = []
        for g in range(GMLP_GROUPS):
            vg = vn[c * GMLP_CHUNK:(c + 1) * GMLP_CHUNK, g * GMLP_GROUP_DIM:(g + 1) * GMLP_GROUP_DIM]
            parts.append(jnp.dot(wsp[g], vg, preferred_element_type=F32) + bsb[g])
        sp_rows.append(jnp.concatenate(parts, axis=1))
    sp = jnp.concatenate(sp_rows, axis=0)
    ya_in = (gu * sp * sza).astype(BF16)
    y_a = jnp.dot(ya_in, _unpack(wba_ref), preferred_element_type=F32)

    def level_operand(hs):
        if hs >= SUBLANES:
            shp = (TM // (2 * hs), 2 * hs, D_MODEL)
            b3, q3, k3 = b.reshape(shp), qh.reshape(shp), kk.reshape(shp)
            r = b3[:, hs - 1:hs, :]
            xk = k3[:, :hs, :] * jnp.exp2(r - b3[:, :hs, :])
            xq = q3[:, hs:, :] * jnp.exp2(b3[:, hs:, :] - r)
            return xk, xq
        shp = (TM // SUBLANES, SUBLANES, D_MODEL)
        b8, q8, k8 = b.reshape(shp), qh.reshape(shp), kk.reshape(shp)
        row8 = lax.broadcasted_iota(jnp.int32, (1, SUBLANES, D_MODEL), 1)
        if hs == 4:
            r = jnp.broadcast_to(b8[:, 3:4, :], shp)
        else:
            r = jnp.where(row8 < 4, jnp.broadcast_to(b8[:, 1:2, :], shp), jnp.broadcast_to(b8[:, 5:6, :], shp))
        is_q = (row8 % (2 * hs)) >= hs
        d = (b8 - r) * jnp.where(is_q, 1.0, -1.0)
        x = jnp.where(is_q, q8, k8) * jnp.exp2(d)
        return x.reshape(TM, D_MODEL).astype(BF16)

    def as_rows(x3):
        return x3.reshape(TOP_HALF, D_MODEL).astype(BF16)

    xk_top, xq_top = (as_rows(x3) for x3 in level_operand(TOP_HALF))
    block_levels = []
    for hs in MXU_HALVES:
        if hs >= SUBLANES:
            xk, xq = level_operand(hs)
            x_all = jnp.concatenate([xk, xq], axis=1).reshape(TM, D_MODEL).astype(BF16)
            block_levels.append((hs, x_all, as_rows(xq)))
    row_levels = [(hs, level_operand(hs)) for hs in MXU_HALVES if hs < SUBLANES]

    b_last = b[TM - 1:TM, :]
    qs = (qh * jnp.exp2(b)).astype(BF16)
    ks = (kk * jnp.exp2(b_last - b)).astype(BF16)
    dl = jnp.exp2(b_last)

    rows = lax.broadcasted_iota(jnp.int32, (TM, D_MODEL), 0)
    w0 = qh * kk
    w1 = jnp.where((rows & 1) == 1, qh * pltpu.roll(kk, 1, 0) * forget, 0.0)

    lane = lax.broadcasted_iota(jnp.int32, (SUBLANES, LANES), 1)
    srow = lax.broadcasted_iota(jnp.int32, (SUBLANES, LANES), 0)
    mask_cache = {}

    def lane_range(off, n):
        key = (off, n)
        if key not in mask_cache:
            mask_cache[key] = (lane >= off) & (lane < off + n)
        return mask_cache[key]

    def diag_block(hs, off):
        key = (hs, off, 'd')
        if key not in mask_cache:
            c = lane - off
            mask_cache[key] = ((c >= 0) & (c < SUBLANES) & ((c ^ srow) < 2 * hs)
                               & ((srow & hs) != 0) & ((c & hs) == 0))
        return mask_cache[key]

    def diagonal(off, below):
        key = (off, below, 'g')
        if key not in mask_cache:
            m = lane == (srow + (off - below))
            mask_cache[key] = m & ((srow & 1) == 1) if below else m
        return mask_cache[key]

    n_reg = TM // SUBLANES
    zero = jnp.zeros((SUBLANES, LANES), F32)
    hng = hng_ref[...]
    o_heads = []

    def head_scores(hd):
        sl = slice(hd * HEAD_DIM, (hd + 1) * HEAD_DIM)
        a_t = [[zero] * n_reg for _ in range(TM // LANES)]
        p_top = _dot_nt(xq_top[:, sl], xk_top[:, sl])
        for i in range(TOP_HALF // SUBLANES):
            a_t[0][TOP_HALF // SUBLANES + i] = p_top[SUBLANES * i:SUBLANES * (i + 1)]
        for hs, x_all, xqc in block_levels:
            pc = _dot_nt(xqc[:, sl], x_all[:, sl])
            for j in range(TM // (2 * hs)):
                half, off = divmod(2 * hs * j, LANES)
                m = lane_range(off, hs)
                for kp in range(hs // SUBLANES):
                    src = j * hs + SUBLANES * kp
                    idx = (2 * hs * j + hs) // SUBLANES + kp
                    piece = pc[src:src + SUBLANES, half * LANES:(half + 1) * LANES]
                    a_t[half][idx] = jnp.where(m, piece, a_t[half][idx])
        for hs, xl in row_levels:
            xh = xl[:, sl]
            pf = _dot_nt(xh, xh)
            for i in range(n_reg):
                half, off = divmod(SUBLANES * i, LANES)
                piece = pf[SUBLANES * i:SUBLANES * (i + 1), half * LANES:(half + 1) * LANES]
                a_t[half][i] = jnp.where(diag_block(hs, off), piece, a_t[half][i])
        s0 = jnp.sum(w0[:, sl], axis=-1, keepdims=True)
        s1 = jnp.sum(w1[:, sl], axis=-1, keepdims=True)
        for i in range(n_reg):
            half, off = divmod(SUBLANES * i, LANES)
            rs = slice(SUBLANES * i, SUBLANES * (i + 1))
            below = jnp.where(diagonal(off, 1), jnp.broadcast_to(s1[rs], (SUBLANES, LANES)), a_t[half][i])
            a_t[half][i] = jnp.where(diagonal(off, 0), jnp.broadcast_to(s0[rs], (SUBLANES, LANES)), below)
        a = jnp.concatenate([jnp.concatenate(col, axis=0) for col in a_t], axis=1)
        return a.astype(BF16)

    def head_output(hd, a):
        sl = slice(hd * HEAD_DIM, (hd + 1) * HEAD_DIM)
        vh = vv[:, sl]
        st = st_ref[hd]
        o_h = jnp.dot(a, vh, preferred_element_type=F32) + _dot_nt(qs[:, sl], st.astype(BF16))
        st_ref[hd] = st * dl[:, sl] + _dot_tn(vh, ks[:, sl])
        o_heads.append(_rms(o_h, hng[:, sl]))

    a_next = head_scores(0)
    for hd in range(HEADS):
        a_cur = a_next
        if hd + 1 < HEADS:
            a_next = head_scores(hd + 1)
        head_output(hd, a_cur)
    o_all = jnp.concatenate(o_heads, axis=1)
    yb_in = (o_all * szb).astype(BF16)

    halves = [slice(i * TOP_HALF, (i + 1) * TOP_HALF) for i in range(2)]
    y_b = [jnp.dot(yb_in[r], _unpack(wbb_ref), preferred_element_type=F32) for r in halves]
    ga = _sigmoid(proj(7))
    gb = _sigmoid(proj(8))
    mo = [jnp.dot((ga[r] * y_a[r] + gb[r] * y_b[i]).astype(BF16), _unpack(wout_ref), preferred_element_type=F32)
          for i, r in enumerate(halves)]
    x1 = [x[r] + _rms(mo[i], gpost_ref[...]) for i, r in enumerate(halves)]
    gl = [jnp.dot(x1[i].astype(BF16), _unpack(wpg_ref), preferred_element_type=F32) + bpg_ref[...] for i in range(2)]
    for i, r in enumerate(halves):
        o_ref[r, :] = x1[i] + _rms(e_ple[r] * _sigmoid(gl[i]), gple_ref[...])


def _pack_body(w_ref, o_ref):
    o_ref[...] = pltpu.bitcast(w_ref[...].astype(BF16), jnp.uint32)


def _pack_rows(w):
    k, n = w.shape
    bk, bn = min(k, PACK_BLOCK_ROWS), min(n, PACK_BLOCK_COLS)
    assert k % bk == 0 and n % bn == 0
    return pl.pallas_call(
        _pack_body,
        out_shape=jax.ShapeDtypeStruct((k // 2, n), jnp.uint32),
        grid=(k // bk, n // bn),
        in_specs=[pl.BlockSpec((bk, bn), lambda i, j: (i, j))],
        out_specs=pl.BlockSpec((bk // 2, bn), lambda i, j: (i, j)),
        name="pack_weight_rows",
    )(w)


def _const_spec(shape):
    nd = len(shape)
    return pl.BlockSpec(shape, lambda b, t: (0,) * nd, pipeline_mode=pl.Buffered(1))


@jax.jit
def kernel(x, p, w_in, gmlp_ln_g, gmlp_ln_b, gmlp_w_s, gmlp_b_s, hgrn_lb_logits, hgrn_norm_g, w_branch_a, w_branch_b, w_out, g_pre, g_post, w_ple, w_ple_gate, b_ple_gate, g_ple):
    bsz, t, d = x.shape
    assert d == D_MODEL and t % TM == 0 and p.shape[0] == 1
    row = lambda a: a.reshape(1, -1).astype(F32)

    consts = (
        _pack_rows(w_in[0]),
        row(gmlp_ln_g[0]), row(gmlp_ln_b[0]),
        gmlp_w_s[0].astype(BF16),
        gmlp_b_s[0].T.astype(F32),
        hgrn_lb_logits.astype(F32),
        row(hgrn_norm_g[0]),
        _pack_rows(w_branch_a[0]), _pack_rows(w_branch_b[0]), _pack_rows(w_out[0]),
        row(g_pre[0]), row(g_post[0]),
        _pack_rows(w_ple[0]), _pack_rows(w_ple_gate[0]),
        row(b_ple_gate[0]), row(g_ple[0]),
    )
    tile = lambda width: pl.BlockSpec((None, TM, width), lambda b, t: (b, t, 0))
    return pl.pallas_call(
        _body,
        out_shape=jax.ShapeDtypeStruct(x.shape, x.dtype),
        grid=(bsz, t // TM),
        in_specs=[tile(D_MODEL), tile(PLE_DIM)] + [_const_spec(c.shape) for c in consts],
        out_specs=tile(D_MODEL),
        scratch_shapes=[pltpu.VMEM((HEADS, HEAD_DIM, HEAD_DIM), F32)],
        compiler_params=pltpu.CompilerParams(
            dimension_semantics=("arbitrary", "arbitrary"),
            vmem_limit_bytes=VMEM_LIMIT_BYTES,
        ),
        name="hybrid_block",
    )(x, p[0], *consts)
```

```python
import jax
import jax.numpy as jnp
from jax import lax
from jax.experimental import pallas as pl
from jax.experimental.pallas import tpu as pltpu

F32 = jnp.float32
BF16 = jnp.bfloat16

D_MODEL = 1024
PLE_DIM = 256
EPS = 1e-6
LOG2E = 1.4426950408889634
GMLP_GROUPS = 4
GMLP_GROUP_DIM = 256
GMLP_CHUNK = 128
HEADS = 8
HEAD_DIM = 128
SUBLANES = 8
LANES = 128

TM = 256
TOP_HALF = TM // 2
MXU_HALVES = (64, 32, 16, 8, 4, 2)
PACK_BLOCK_ROWS = 512
PACK_BLOCK_COLS = 1024
VMEM_LIMIT_BYTES = 60000 * 1024


def _rms(y, g):
    return y * lax.rsqrt(jnp.mean(y * y, axis=-1, keepdims=True) + EPS) * g


def _gelu(y):
    return 0.5 * y * (1.0 + lax.erf(y * (2.0 ** -0.5)))


def _sigmoid(y):
    return jax.nn.sigmoid(y)


def _silu(y):
    return y * _sigmoid(y)


def _unpack(w_ref, c0=None, c1=None):
    w = w_ref[...] if c0 is None else w_ref[:, c0:c1]
    return pltpu.bitcast(w, BF16)


def _dot_nt(a, b):
    return lax.dot_general(a, b, (((1,), (1,)), ((), ())), preferred_element_type=F32)


def _dot_tn(a, b):
    return lax.dot_general(a, b, (((0,), (0,)), ((), ())), preferred_element_type=F32)


def _body(x_ref, p_ref, win_ref, lng_ref, lnb_ref, wsp_ref, bs_ref, lbl_ref, hng_ref,
          wba_ref, wbb_ref, wout_ref, gpre_ref, gpost_ref, wple_ref, wpg_ref, bpg_ref, gple_ref,
          o_ref, st_ref):
    @pl.when(pl.program_id(1) == 0)
    def _():
        st_ref[...] = jnp.zeros_like(st_ref)

    e_ple = jnp.dot(p_ref[...].astype(BF16), _unpack(wple_ref), preferred_element_type=F32)
    x = x_ref[...]
    h = _rms(x, gpre_ref[...]).astype(BF16)

    chunk_cols = 256
    n_chunks = D_MODEL // chunk_cols

    def proj_chunk(j, c):
        c0 = j * D_MODEL + c * chunk_cols
        return jnp.dot(h, _unpack(win_ref, c0, c0 + chunk_cols), preferred_element_type=F32)

    def cols(c):
        return slice(c * chunk_cols, (c + 1) * chunk_cols)

    def row_cat(parts):
        return jnp.concatenate(parts, axis=1)


    lbl = lbl_ref[...]
    le = jnp.exp(lbl - jnp.max(lbl, axis=0, keepdims=True))
    lb = le[0:1, :] / jnp.sum(le, axis=0, keepdims=True)
    f_raw = [proj_chunk(4, c) for c in range(n_chunks)]
    ti = lax.broadcasted_iota(jnp.int32, (TM, TM), 0)
    si = lax.broadcasted_iota(jnp.int32, (TM, TM), 1)
    tri = jnp.where(ti >= si, 1.0, 0.0).astype(BF16)
    forget_c, b_c, qh_c = [], [], []
    for c in range(n_chunks):
        q_raw = proj_chunk(3, c)
        fg = lb[:, cols(c)] + (1.0 - lb[:, cols(c)]) * _sigmoid(f_raw[c])
        lf = jnp.log(fg)
        lf_hi = lf.astype(BF16)
        lf_lo = (lf - lf_hi.astype(F32)).astype(BF16)
        bc = jnp.dot(tri, lf_hi, preferred_element_type=F32) + jnp.dot(tri, lf_lo, preferred_element_type=F32)
        forget_c.append(fg)
        b_c.append(bc * LOG2E)
        qh_c.append(_silu(q_raw) * (HEAD_DIM ** -0.5))
    forget = row_cat(forget_c)
    kk = 1.0 - forget
    b = row_cat(b_c)
    qh = row_cat(qh_c)

    gv_c = [_gelu(proj_chunk(1, c)) for c in range(n_chunks)]
    vf = row_cat([proj_chunk(5, c) for c in range(n_chunks)])
    vv = vf.astype(BF16)
    mu = sum(jnp.sum(g, axis=-1, keepdims=True) for g in gv_c) * (1.0 / D_MODEL)
    vc_c = [g - mu for g in gv_c]
    var = sum(jnp.sum(v * v, axis=-1, keepdims=True) for v in vc_c) * (1.0 / D_MODEL)
    rstd = lax.rsqrt(var + EPS)

    r128 = lax.broadcasted_iota(jnp.int32, (GMLP_CHUNK, GMLP_CHUNK), 0)
    c128 = lax.broadcasted_iota(jnp.int32, (GMLP_CHUNK, GMLP_CHUNK), 1)
    gu_c, sp_c = [], []
    for g in range(GMLP_GROUPS):
        gu_c.append(_gelu(proj_chunk(0, g)))
        vn = (vc_c[g] * rstd * lng_ref[:, cols(g)] + lnb_ref[:, cols(g)]).astype(BF16)
        wg = jnp.where(r128 >= c128, wsp_ref[g], jnp.zeros((), BF16))
        bias = jnp.broadcast_to(bs_ref[:, g:g + 1], (GMLP_CHUNK, GMLP_GROUP_DIM))
        sp_c.append(jnp.concatenate(
            [jnp.dot(wg, vn[r * GMLP_CHUNK:(r + 1) * GMLP_CHUNK], preferred_element_type=F32) + bias
             for r in range(TM // GMLP_CHUNK)], axis=0))

    ya_parts = []

    def branch_a_chunk(c, za_raw):
        ya_in = (gu_c[c] * sp_c[c] * _silu(za_raw)).astype(BF16)
        w = pltpu.bitcast(wba_ref[c * chunk_cols // 2:(c + 1) * chunk_cols // 2, :], BF16)
        ya_parts.append(jnp.dot(ya_in, w, preferred_element_type=F32))

    za_raw = [proj_chunk(2, 0), proj_chunk(2, 1)]
    branch_a_chunk(0, za_raw[0])
    za_raw.append(proj_chunk(2, 2))
    branch_a_chunk(1, za_raw[1])
    za_raw.append(proj_chunk(2, 3))
    branch_a_chunk(2, za_raw[2])

    def level_operand(hs):
        if hs >= SUBLANES:
            shp = (TM // (2 * hs), 2 * hs, D_MODEL)
            b3, q3, k3 = b.reshape(shp), qh.reshape(shp), kk.reshape(shp)
            r = b3[:, hs - 1:hs, :]
            xk = k3[:, :hs, :] * jnp.exp2(r - b3[:, :hs, :])
            xq = q3[:, hs:, :] * jnp.exp2(b3[:, hs:, :] - r)
            return xk, xq
        shp = (TM // SUBLANES, SUBLANES, D_MODEL)
        b8, q8, k8 = b.reshape(shp), qh.reshape(shp), kk.reshape(shp)
        row8 = lax.broadcasted_iota(jnp.int32, (1, SUBLANES, D_MODEL), 1)
        if hs == 4:
            r = jnp.broadcast_to(b8[:, 3:4, :], shp)
        else:
            r = jnp.where(row8 < 4, jnp.broadcast_to(b8[:, 1:2, :], shp), jnp.broadcast_to(b8[:, 5:6, :], shp))
        is_q = (row8 % (2 * hs)) >= hs
        d = (b8 - r) * jnp.where(is_q, 1.0, -1.0)
        x = jnp.where(is_q, q8, k8) * jnp.exp2(d)
        return x.reshape(TM, D_MODEL).astype(BF16)

    def as_rows(x3):
        return x3.reshape(TOP_HALF, D_MODEL).astype(BF16)

    xk_top, xq_top = (as_rows(x3) for x3 in level_operand(TOP_HALF))
    block_levels = []
    for hs in MXU_HALVES:
        if hs >= SUBLANES:
            xk, xq = level_operand(hs)
            x_all = jnp.concatenate([xk, xq], axis=1).reshape(TM, D_MODEL).astype(BF16)
            block_levels.append((hs, x_all, as_rows(xq)))
    row_levels = [(hs, level_operand(hs)) for hs in MXU_HALVES if hs < SUBLANES]

    b_last = b[TM - 1:TM, :]
    qs = (qh * jnp.exp2(b)).astype(BF16)
    ks = (kk * jnp.exp2(b_last - b)).astype(BF16)
    dl = jnp.exp2(b_last)

    rows = lax.broadcasted_iota(jnp.int32, (TM, D_MODEL), 0)
    w0 = qh * kk
    w1 = jnp.where((rows & 1) == 1, qh * pltpu.roll(kk, 1, 0) * forget, 0.0)

    lane = lax.broadcasted_iota(jnp.int32, (SUBLANES, LANES), 1)
    srow = lax.broadcasted_iota(jnp.int32, (SUBLANES, LANES), 0)
    mask_cache = {}

    def lane_range(off, n):
        key = (off, n)
        if key not in mask_cache:
            mask_cache[key] = (lane >= off) & (lane < off + n)
        return mask_cache[key]

    def diag_block(hs, off):
        key = (hs, off, 'd')
        if key not in mask_cache:
            c = lane - off
            mask_cache[key] = ((c >= 0) & (c < SUBLANES) & ((c ^ srow) < 2 * hs)
                               & ((srow & hs) != 0) & ((c & hs) == 0))
        return mask_cache[key]

    def diagonal(off, below):
        key = (off, below, 'g')
        if key not in mask_cache:
            m = lane == (srow + (off - below))
            mask_cache[key] = m & ((srow & 1) == 1) if below else m
        return mask_cache[key]

    pending = [('a', n_chunks - 1)] + [(j, c) for j in (6, 7, 8) for c in range(n_chunks)]
    chunks = {6: [], 7: [], 8: []}

    def filler():
        if pending:
            j, c = pending.pop(0)
            if j == 'a':
                branch_a_chunk(c, za_raw[c])
            else:
                chunks[j].append(proj_chunk(j, c))

    n_reg = TM // SUBLANES
    zero = jnp.zeros((SUBLANES, LANES), F32)
    hng = hng_ref[...]
    o_heads = []

    def head_scores(hd):
        sl = slice(hd * HEAD_DIM, (hd + 1) * HEAD_DIM)
        a_t = [[zero] * n_reg for _ in range(TM // LANES)]
        p_top = _dot_nt(xq_top[:, sl], xk_top[:, sl])
        for i in range(TOP_HALF // SUBLANES):
            a_t[0][TOP_HALF // SUBLANES + i] = p_top[SUBLANES * i:SUBLANES * (i + 1)]
        for hs, x_all, xqc in block_levels:
            pc = _dot_nt(xqc[:, sl], x_all[:, sl])
            for j in range(TM // (2 * hs)):
                half, off = divmod(2 * hs * j, LANES)
                m = lane_range(off, hs)
                for kp in range(hs // SUBLANES):
                    src = j * hs + SUBLANES * kp
                    idx = (2 * hs * j + hs) // SUBLANES + kp
                    piece = pc[src:src + SUBLANES, half * LANES:(half + 1) * LANES]
                    a_t[half][idx] = jnp.where(m, piece, a_t[half][idx])
        filler()
        for hs, xl in row_levels:
            xh = xl[:, sl]
            pf = _dot_nt(xh, xh)
            for i in range(n_reg):
                half, off = divmod(SUBLANES * i, LANES)
                piece = pf[SUBLANES * i:SUBLANES * (i + 1), half * LANES:(half + 1) * LANES]
                a_t[half][i] = jnp.where(diag_block(hs, off), piece, a_t[half][i])
        s0 = jnp.sum(w0[:, sl], axis=-1, keepdims=True)
        s1 = jnp.sum(w1[:, sl], axis=-1, keepdims=True)
        for i in range(n_reg):
            half, off = divmod(SUBLANES * i, LANES)
            rs = slice(SUBLANES * i, SUBLANES * (i + 1))
            below = jnp.where(diagonal(off, 1), jnp.broadcast_to(s1[rs], (SUBLANES, LANES)), a_t[half][i])
            a_t[half][i] = jnp.where(diagonal(off, 0), jnp.broadcast_to(s0[rs], (SUBLANES, LANES)), below)
        a = jnp.concatenate([jnp.concatenate(col, axis=0) for col in a_t], axis=1)
        return a.astype(BF16)

    def head_output(hd, a):
        sl = slice(hd * HEAD_DIM, (hd + 1) * HEAD_DIM)
        vh = vv[:, sl]
        st = st_ref[hd]
        o_h = jnp.dot(a, vh, preferred_element_type=F32) + _dot_nt(qs[:, sl], st.astype(BF16))
        st_ref[hd] = st * dl[:, sl] + _dot_tn(vh, ks[:, sl])
        o_heads.append(_rms(o_h, hng[:, sl]))
        if hd % 2:
            filler()

    a_next = head_scores(0)
    for hd in range(HEADS):
        a_cur = a_next
        if hd + 1 < HEADS:
            a_next = head_scores(hd + 1)
        head_output(hd, a_cur)
    o_all = jnp.concatenate(o_heads, axis=1)
    while pending:
        filler()
    szb = _silu(row_cat(chunks[6]))
    y_a = ya_parts[0] + ya_parts[1] + ya_parts[2] + ya_parts[3]
    yb_in = (o_all * szb).astype(BF16)

    halves = [slice(i * TOP_HALF, (i + 1) * TOP_HALF) for i in range(2)]
    y_b = [jnp.dot(yb_in[r], _unpack(wbb_ref), preferred_element_type=F32) for r in halves]
    ga = _sigmoid(jnp.concatenate(chunks[7], axis=1))
    gb = _sigmoid(jnp.concatenate(chunks[8], axis=1))
    mo = [jnp.dot((ga[r] * y_a[r] + gb[r] * y_b[i]).astype(BF16), _unpack(wout_ref), preferred_element_type=F32)
          for i, r in enumerate(halves)]
    x1 = [x[r] + _rms(mo[i], gpost_ref[...]) for i, r in enumerate(halves)]
    gl = [jnp.dot(x1[i].astype(BF16), _unpack(wpg_ref), preferred_element_type=F32) + bpg_ref[...] for i in range(2)]
    for i, r in enumerate(halves):
        o_ref[r, :] = x1[i] + _rms(e_ple[r] * _sigmoid(gl[i]), gple_ref[...])


def _pack_body(w_ref, o_ref):
    o_ref[...] = pltpu.bitcast(w_ref[...].astype(BF16), jnp.uint32)


def _pack_rows(w):
    k, n = w.shape
    bk, bn = min(k, PACK_BLOCK_ROWS), min(n, PACK_BLOCK_COLS)
    assert k % bk == 0 and n % bn == 0
    return pl.pallas_call(
        _pack_body,
        out_shape=jax.ShapeDtypeStruct((k // 2, n), jnp.uint32),
        grid=(k // bk, n // bn),
        in_specs=[pl.BlockSpec((bk, bn), lambda i, j: (i, j))],
        out_specs=pl.BlockSpec((bk // 2, bn), lambda i, j: (i, j)),
        name="pack_weight_rows",
    )(w)


def _const_spec(shape):
    nd = len(shape)
    return pl.BlockSpec(shape, lambda b, t: (0,) * nd, pipeline_mode=pl.Buffered(1))


@jax.jit
def kernel(x, p, w_in, gmlp_ln_g, gmlp_ln_b, gmlp_w_s, gmlp_b_s, hgrn_lb_logits, hgrn_norm_g, w_branch_a, w_branch_b, w_out, g_pre, g_post, w_ple, w_ple_gate, b_ple_gate, g_ple):
    bsz, t, d = x.shape
    assert d == D_MODEL and t % TM == 0 and p.shape[0] == 1
    row = lambda a: a.reshape(1, -1).astype(F32)

    consts = (
        _pack_rows(w_in[0]),
        row(gmlp_ln_g[0]), row(gmlp_ln_b[0]),
        gmlp_w_s[0].astype(BF16),
        gmlp_b_s[0].T.astype(F32),
        hgrn_lb_logits.astype(F32),
        row(hgrn_norm_g[0]),
        _pack_rows(w_branch_a[0]), _pack_rows(w_branch_b[0]), _pack_rows(w_out[0]),
        row(g_pre[0]), row(g_post[0]),
        _pack_rows(w_ple[0]), _pack_rows(w_ple_gate[0]),
        row(b_ple_gate[0]), row(g_ple[0]),
    )
    tile = lambda width: pl.BlockSpec((None, TM, width), lambda b, t: (b, t, 0))
    return pl.pallas_call(
        _body,
        out_shape=jax.ShapeDtypeStruct(x.shape, x.dtype),
        grid=(bsz, t // TM),
        in_specs=[tile(D_MODEL), tile(PLE_DIM)] + [_const_spec(c.shape) for c in consts],
        out_specs=tile(D_MODEL),
        scratch_shapes=[pltpu.VMEM((HEADS, HEAD_DIM, HEAD_DIM), F32)],
        compiler_params=pltpu.CompilerParams(
            dimension_semantics=("arbitrary", "arbitrary"),
            vmem_limit_bytes=VMEM_LIMIT_BYTES,
        ),
        name="hybrid_block",
    )(x, p[0], *consts)
```

```python
import jax
import jax.numpy as jnp
from jax import lax
from jax.experimental import pallas as pl
from jax.experimental.pallas import tpu as pltpu

F32 = jnp.float32
BF16 = jnp.bfloat16

D_MODEL = 1024
PLE_DIM = 256
EPS = 1e-6
LOG2E = 1.4426950408889634
GMLP_GROUPS = 4
GMLP_GROUP_DIM = 256
GMLP_CHUNK = 128
HEADS = 8
HEAD_DIM = 128
SUBLANES = 8
LANES = 128

TM = 256
TILES_PER_STEP = 2
TOP_HALF = TM // 2
MXU_HALVES = (64, 32, 16, 8, 4, 2)
PACK_BLOCK_ROWS = 512
PACK_BLOCK_COLS = 1024
VMEM_LIMIT_BYTES = 60000 * 1024


def _rms(y, g):
    return y * lax.rsqrt(jnp.mean(y * y, axis=-1, keepdims=True) + EPS) * g


def _gelu(y):
    return 0.5 * y * (1.0 + lax.erf(y * (2.0 ** -0.5)))


def _sigmoid(y):
    return jax.nn.sigmoid(y)


def _silu(y):
    return y * _sigmoid(y)


def _unpack(w_ref, c0=None, c1=None):
    w = w_ref[...] if c0 is None else w_ref[:, c0:c1]
    return pltpu.bitcast(w, BF16)


def _dot_nt(a, b):
    return lax.dot_general(a, b, (((1,), (1,)), ((), ())), preferred_element_type=F32)


def _dot_tn(a, b):
    return lax.dot_general(a, b, (((0,), (0,)), ((), ())), preferred_element_type=F32)


def _body(x_ref, p_ref, win_ref, lng_ref, lnb_ref, wsp_ref, bs_ref, lbl_ref, hng_ref,
          wba_ref, wbb_ref, wout_ref, gpre_ref, gpost_ref, wple_ref, wpg_ref, bpg_ref, gple_ref,
          o_ref, st_ref):
    @pl.when(pl.program_id(1) == 0)
    def _():
        st_ref[...] = jnp.zeros_like(st_ref)

    refs = (win_ref, lng_ref, lnb_ref, wsp_ref, bs_ref, lbl_ref, hng_ref,
            wba_ref, wbb_ref, wout_ref, gpre_ref, gpost_ref, wple_ref, wpg_ref, bpg_ref, gple_ref, st_ref)
    for sub in range(TILES_PER_STEP):
        _tile(x_ref, p_ref, o_ref, sub * TM, *refs)


def _tile(x_ref, p_ref, o_ref, r0, win_ref, lng_ref, lnb_ref, wsp_ref, bs_ref, lbl_ref, hng_ref,
          wba_ref, wbb_ref, wout_ref, gpre_ref, gpost_ref, wple_ref, wpg_ref, bpg_ref, gple_ref, st_ref):

    e_ple = jnp.dot(p_ref[r0:r0 + TM, :].astype(BF16), _unpack(wple_ref), preferred_element_type=F32)
    x = x_ref[r0:r0 + TM, :]
    h = _rms(x, gpre_ref[...]).astype(BF16)

    chunk_cols = 256
    n_chunks = D_MODEL // chunk_cols

    def proj_chunk(j, c):
        c0 = j * D_MODEL + c * chunk_cols
        return jnp.dot(h, _unpack(win_ref, c0, c0 + chunk_cols), preferred_element_type=F32)

    def cols(c):
        return slice(c * chunk_cols, (c + 1) * chunk_cols)

    def row_cat(parts):
        return jnp.concatenate(parts, axis=1)


    lbl = lbl_ref[...]
    le = jnp.exp(lbl - jnp.max(lbl, axis=0, keepdims=True))
    lb = le[0:1, :] / jnp.sum(le, axis=0, keepdims=True)
    f_raw = [proj_chunk(4, c) for c in range(n_chunks)]
    ti = lax.broadcasted_iota(jnp.int32, (TM, TM), 0)
    si = lax.broadcasted_iota(jnp.int32, (TM, TM), 1)
    tri = jnp.where(ti >= si, 1.0, 0.0).astype(BF16)
    forget_c, b_c, qh_c = [], [], []
    for c in range(n_chunks):
        q_raw = proj_chunk(3, c)
        fg = lb[:, cols(c)] + (1.0 - lb[:, cols(c)]) * _sigmoid(f_raw[c])
        lf = jnp.log(fg)
        lf_hi = lf.astype(BF16)
        lf_lo = (lf - lf_hi.astype(F32)).astype(BF16)
        bc = jnp.dot(tri, lf_hi, preferred_element_type=F32) + jnp.dot(tri, lf_lo, preferred_element_type=F32)
        forget_c.append(fg)
        b_c.append(bc * LOG2E)
        qh_c.append(_silu(q_raw) * (HEAD_DIM ** -0.5))
    forget = row_cat(forget_c)
    kk = 1.0 - forget
    b = row_cat(b_c)
    qh = row_cat(qh_c)

    gv_c = [_gelu(proj_chunk(1, c)) for c in range(n_chunks)]
    vf = row_cat([proj_chunk(5, c) for c in range(n_chunks)])
    vv = vf.astype(BF16)
    za_early = [proj_chunk(2, 0), proj_chunk(2, 1)]
    mu = sum(jnp.sum(g, axis=-1, keepdims=True) for g in gv_c) * (1.0 / D_MODEL)
    vc_c = [g - mu for g in gv_c]
    var = sum(jnp.sum(v * v, axis=-1, keepdims=True) for v in vc_c) * (1.0 / D_MODEL)
    rstd = lax.rsqrt(var + EPS)

    r128 = lax.broadcasted_iota(jnp.int32, (GMLP_CHUNK, GMLP_CHUNK), 0)
    c128 = lax.broadcasted_iota(jnp.int32, (GMLP_CHUNK, GMLP_CHUNK), 1)
    gu_c, sp_c = [], []
    for g in range(GMLP_GROUPS):
        gu_c.append(_gelu(proj_chunk(0, g)))
        vn = (vc_c[g] * rstd * lng_ref[:, cols(g)] + lnb_ref[:, cols(g)]).astype(BF16)
        wg = jnp.where(r128 >= c128, wsp_ref[g], jnp.zeros((), BF16))
        bias = jnp.broadcast_to(bs_ref[:, g:g + 1], (GMLP_CHUNK, GMLP_GROUP_DIM))
        sp_c.append(jnp.concatenate(
            [jnp.dot(wg, vn[r * GMLP_CHUNK:(r + 1) * GMLP_CHUNK], preferred_element_type=F32) + bias
             for r in range(TM // GMLP_CHUNK)], axis=0))

    ya_parts = []

    def branch_a_chunk(c, za_raw):
        ya_in = (gu_c[c] * sp_c[c] * _silu(za_raw)).astype(BF16)
        w = pltpu.bitcast(wba_ref[c * chunk_cols // 2:(c + 1) * chunk_cols // 2, :], BF16)
        ya_parts.append(jnp.dot(ya_in, w, preferred_element_type=F32))

    za_raw = za_early
    branch_a_chunk(0, za_raw[0])
    za_raw.append(proj_chunk(2, 2))
    branch_a_chunk(1, za_raw[1])
    za_raw.append(proj_chunk(2, 3))
    branch_a_chunk(2, za_raw[2])

    def level_operand(hs):
        if hs >= SUBLANES:
            shp = (TM // (2 * hs), 2 * hs, D_MODEL)
            b3, q3, k3 = b.reshape(shp), qh.reshape(shp), kk.reshape(shp)
            r = b3[:, hs - 1:hs, :]
            xk = k3[:, :hs, :] * jnp.exp2(r - b3[:, :hs, :])
            xq = q3[:, hs:, :] * jnp.exp2(b3[:, hs:, :] - r)
            return xk, xq
        shp = (TM // SUBLANES, SUBLANES, D_MODEL)
        b8, q8, k8 = b.reshape(shp), qh.reshape(shp), kk.reshape(shp)
        row8 = lax.broadcasted_iota(jnp.int32, (1, SUBLANES, D_MODEL), 1)
        if hs == 4:
            r = jnp.broadcast_to(b8[:, 3:4, :], shp)
        else:
            r = jnp.where(row8 < 4, jnp.broadcast_to(b8[:, 1:2, :], shp), jnp.broadcast_to(b8[:, 5:6, :], shp))
        is_q = (row8 % (2 * hs)) >= hs
        d = (b8 - r) * jnp.where(is_q, 1.0, -1.0)
        x = jnp.where(is_q, q8, k8) * jnp.exp2(d)
        return x.reshape(TM, D_MODEL).astype(BF16)

    def as_rows(x3):
        return x3.reshape(TOP_HALF, D_MODEL).astype(BF16)

    xk_top, xq_top = (as_rows(x3) for x3 in level_operand(TOP_HALF))
    block_levels = []
    for hs in MXU_HALVES:
        if hs >= SUBLANES:
            xk, xq = level_operand(hs)
            x_all = jnp.concatenate([xk, xq], axis=1).reshape(TM, D_MODEL).astype(BF16)
            block_levels.append((hs, x_all, as_rows(xq)))
    row_levels = [(hs, level_operand(hs)) for hs in MXU_HALVES if hs < SUBLANES]

    b_last = b[TM - 1:TM, :]
    qs = (qh * jnp.exp2(b)).astype(BF16)
    ks = (kk * jnp.exp2(b_last - b)).astype(BF16)
    dl = jnp.exp2(b_last)

    rows = lax.broadcasted_iota(jnp.int32, (TM, D_MODEL), 0)
    w0 = qh * kk
    w1 = jnp.where((rows & 1) == 1, qh * pltpu.roll(kk, 1, 0) * forget, 0.0)

    lane = lax.broadcasted_iota(jnp.int32, (SUBLANES, LANES), 1)
    srow = lax.broadcasted_iota(jnp.int32, (SUBLANES, LANES), 0)
    mask_cache = {}

    def lane_range(off, n):
        key = (off, n)
        if key not in mask_cache:
            mask_cache[key] = (lane >= off) & (lane < off + n)
        return mask_cache[key]

    def diag_block(hs, off):
        key = (hs, off, 'd')
        if key not in mask_cache:
            c = lane - off
            mask_cache[key] = ((c >= 0) & (c < SUBLANES) & ((c ^ srow) < 2 * hs)
                               & ((srow & hs) != 0) & ((c & hs) == 0))
        return mask_cache[key]

    def diagonal(off, below):
        key = (off, below, 'g')
        if key not in mask_cache:
            m = lane == (srow + (off - below))
            mask_cache[key] = m & ((srow & 1) == 1) if below else m
        return mask_cache[key]

    pending = [('a', n_chunks - 1)] + [(j, c) for j in (6, 7, 8) for c in range(n_chunks)]
    chunks = {6: [], 7: [], 8: []}

    def filler():
        if pending:
            j, c = pending.pop(0)
            if j == 'a':
                branch_a_chunk(c, za_raw[c])
            else:
                chunks[j].append(proj_chunk(j, c))

    n_reg = TM // SUBLANES
    zero = jnp.zeros((SUBLANES, LANES), F32)
    hng = hng_ref[...]
    o_heads = []

    def head_scores(hd):
        sl = slice(hd * HEAD_DIM, (hd + 1) * HEAD_DIM)
        a_t = [[zero] * n_reg for _ in range(TM // LANES)]
        p_top = _dot_nt(xq_top[:, sl], xk_top[:, sl])
        for i in range(TOP_HALF // SUBLANES):
            a_t[0][TOP_HALF // SUBLANES + i] = p_top[SUBLANES * i:SUBLANES * (i + 1)]
        for hs, x_all, xqc in block_levels:
            pc = _dot_nt(xqc[:, sl], x_all[:, sl])
            for j in range(TM // (2 * hs)):
                half, off = divmod(2 * hs * j, LANES)
                m = lane_range(off, hs)
                for kp in range(hs // SUBLANES):
                    src = j * hs + SUBLANES * kp
                    idx = (2 * hs * j + hs) // SUBLANES + kp
                    piece = pc[src:src + SUBLANES, half * LANES:(half + 1) * LANES]
                    a_t[half][idx] = jnp.where(m, piece, a_t[half][idx])
        filler()
        for hs, xl in row_levels:
            xh = xl[:, sl]
            pf = _dot_nt(xh, xh)
            for i in range(n_reg):
                half, off = divmod(SUBLANES * i, LANES)
                piece = pf[SUBLANES * i:SUBLANES * (i + 1), half * LANES:(half + 1) * LANES]
                a_t[half][i] = jnp.where(diag_block(hs, off), piece, a_t[half][i])
        s0 = jnp.sum(w0[:, sl], axis=-1, keepdims=True)
        s1 = jnp.sum(w1[:, sl], axis=-1, keepdims=True)
        for i in range(n_reg):
            half, off = divmod(SUBLANES * i, LANES)
            rs = slice(SUBLANES * i, SUBLANES * (i + 1))
            below = jnp.where(diagonal(off, 1), jnp.broadcast_to(s1[rs], (SUBLANES, LANES)), a_t[half][i])
            a_t[half][i] = jnp.where(diagonal(off, 0), jnp.broadcast_to(s0[rs], (SUBLANES, LANES)), below)
        a = jnp.concatenate([jnp.concatenate(col, axis=0) for col in a_t], axis=1)
        return a.astype(BF16)

    def head_output(hd, a):
        sl = slice(hd * HEAD_DIM, (hd + 1) * HEAD_DIM)
        vh = vv[:, sl]
        st = st_ref[hd]
        o_h = jnp.dot(a, vh, preferred_element_type=F32) + _dot_nt(qs[:, sl], st.astype(BF16))
        st_ref[hd] = st * dl[:, sl] + _dot_tn(vh, ks[:, sl])
        o_heads.append(_rms(o_h, hng[:, sl]))
        if hd % 2:
            filler()

    a_next = head_scores(0)
    for hd in range(HEADS):
        a_cur = a_next
        if hd + 1 < HEADS:
            a_next = head_scores(hd + 1)
        head_output(hd, a_cur)
    o_all = jnp.concatenate(o_heads, axis=1)
    while pending:
        filler()
    szb = _silu(row_cat(chunks[6]))
    y_a = ya_parts[0] + ya_parts[1] + ya_parts[2] + ya_parts[3]
    yb_in = (o_all * szb).astype(BF16)

    halves = [slice(i * TOP_HALF, (i + 1) * TOP_HALF) for i in range(2)]
    y_b = [jnp.dot(yb_in[r], _unpack(wbb_ref), preferred_element_type=F32) for r in halves]
    ga = _sigmoid(jnp.concatenate(chunks[7], axis=1))
    gb = _sigmoid(jnp.concatenate(chunks[8], axis=1))
    mo = [jnp.dot((ga[r] * y_a[r] + gb[r] * y_b[i]).astype(BF16), _unpack(wout_ref), preferred_element_type=F32)
          for i, r in enumerate(halves)]
    x1 = [x[r] + _rms(mo[i], gpost_ref[...]) for i, r in enumerate(halves)]
    gl = [jnp.dot(x1[i].astype(BF16), _unpack(wpg_ref), preferred_element_type=F32) + bpg_ref[...] for i in range(2)]
    for i, r in enumerate(halves):
        o_ref[r0 + r.start:r0 + r.stop, :] = x1[i] + _rms(e_ple[r] * _sigmoid(gl[i]), gple_ref[...])


def _pack_body(w_ref, o_ref):
    o_ref[...] = pltpu.bitcast(w_ref[...].astype(BF16), jnp.uint32)


def _pack_rows(w):
    k, n = w.shape
    bk, bn = min(k, PACK_BLOCK_ROWS), min(n, PACK_BLOCK_COLS)
    assert k % bk == 0 and n % bn == 0
    return pl.pallas_call(
        _pack_body,
        out_shape=jax.ShapeDtypeStruct((k // 2, n), jnp.uint32),
        grid=(k // bk, n // bn),
        in_specs=[pl.BlockSpec((bk, bn), lambda i, j: (i, j))],
        out_specs=pl.BlockSpec((bk // 2, bn), lambda i, j: (i, j)),
        name="pack_weight_rows",
    )(w)


def _const_spec(shape):
    nd = len(shape)
    return pl.BlockSpec(shape, lambda b, t: (0,) * nd, pipeline_mode=pl.Buffered(1))


@jax.jit
def kernel(x, p, w_in, gmlp_ln_g, gmlp_ln_b, gmlp_w_s, gmlp_b_s, hgrn_lb_logits, hgrn_norm_g, w_branch_a, w_branch_b, w_out, g_pre, g_post, w_ple, w_ple_gate, b_ple_gate, g_ple):
    bsz, t, d = x.shape
    assert d == D_MODEL and t % (TM * TILES_PER_STEP) == 0 and p.shape[0] == 1
    row = lambda a: a.reshape(1, -1).astype(F32)

    consts = (
        _pack_rows(w_in[0]),
        row(gmlp_ln_g[0]), row(gmlp_ln_b[0]),
        gmlp_w_s[0].astype(BF16),
        gmlp_b_s[0].T.astype(F32),
        hgrn_lb_logits.astype(F32),
        row(hgrn_norm_g[0]),
        _pack_rows(w_branch_a[0]), _pack_rows(w_branch_b[0]), _pack_rows(w_out[0]),
        row(g_pre[0]), row(g_post[0]),
        _pack_rows(w_ple[0]), _pack_rows(w_ple_gate[0]),
        row(b_ple_gate[0]), row(g_ple[0]),
    )
    tile = lambda width: pl.BlockSpec((None, TM * TILES_PER_STEP, width), lambda b, t: (b, t, 0))
    return pl.pallas_call(
        _body,
        out_shape=jax.ShapeDtypeStruct(x.shape, x.dtype),
        grid=(bsz, t // (TM * TILES_PER_STEP)),
        in_specs=[tile(D_MODEL), tile(PLE_DIM)] + [_const_spec(c.shape) for c in consts],
        out_specs=tile(D_MODEL),
        scratch_shapes=[pltpu.VMEM((HEADS, HEAD_DIM, HEAD_DIM), F32)],
        compiler_params=pltpu.CompilerParams(
            dimension_semantics=("arbitrary", "arbitrary"),
            vmem_limit_bytes=VMEM_LIMIT_BYTES,
        ),
        name="hybrid_block",
    )(x, p[0], *consts)
```

```python
import jax
import jax.numpy as jnp
from jax import lax
from jax.experimental import pallas as pl
from jax.experimental.pallas import tpu as pltpu

F32 = jnp.float32
BF16 = jnp.bfloat16

D_MODEL = 1024
PLE_DIM = 256
EPS = 1e-6
LOG2E = 1.4426950408889634
GMLP_GROUPS = 4
GMLP_GROUP_DIM = 256
GMLP_CHUNK = 128
HEADS = 8
HEAD_DIM = 128
SUBLANES = 8
LANES = 128

TM = 256
TILES_PER_STEP = 4
TOP_HALF = TM // 2
MXU_HALVES = (64, 32, 16, 8, 4, 2)
PACK_BLOCK_ROWS = 512
PACK_BLOCK_COLS = 1024
VMEM_LIMIT_BYTES = 63 * 1024 * 1024


def _rms(y, g):
    return y * lax.rsqrt(jnp.mean(y * y, axis=-1, keepdims=True) + EPS) * g


def _gelu(y):
    return 0.5 * y * (1.0 + lax.erf(y * (2.0 ** -0.5)))


def _sigmoid(y):
    return jax.nn.sigmoid(y)


def _silu(y):
    return y * _sigmoid(y)


def _unpack(w_ref, c0=None, c1=None):
    w = w_ref[...] if c0 is None else w_ref[:, c0:c1]
    return pltpu.bitcast(w, BF16)


def _dot_nt(a, b):
    return lax.dot_general(a, b, (((1,), (1,)), ((), ())), preferred_element_type=F32)


def _dot_tn(a, b):
    return lax.dot_general(a, b, (((0,), (0,)), ((), ())), preferred_element_type=F32)


def _body(x_ref, p_ref, win_ref, lng_ref, lnb_ref, wsp_ref, bs_ref, lbl_ref, hng_ref,
          wba_ref, wbb_ref, wout_ref, gpre_ref, gpost_ref, wple_ref, wpg_ref, bpg_ref, gple_ref,
          o_ref, st_ref):
    @pl.when(pl.program_id(1) == 0)
    def _():
        st_ref[...] = jnp.zeros_like(st_ref)

    refs = (win_ref, lng_ref, lnb_ref, wsp_ref, bs_ref, lbl_ref, hng_ref,
            wba_ref, wbb_ref, wout_ref, gpre_ref, gpost_ref, wple_ref, wpg_ref, bpg_ref, gple_ref, st_ref)
    for sub in range(TILES_PER_STEP):
        _tile(x_ref, p_ref, o_ref, sub * TM, *refs)


def _tile(x_ref, p_ref, o_ref, r0, win_ref, lng_ref, lnb_ref, wsp_ref, bs_ref, lbl_ref, hng_ref,
          wba_ref, wbb_ref, wout_ref, gpre_ref, gpost_ref, wple_ref, wpg_ref, bpg_ref, gple_ref, st_ref):

    e_ple = jnp.dot(p_ref[r0:r0 + TM, :].astype(BF16), _unpack(wple_ref), preferred_element_type=F32)
    x = x_ref[r0:r0 + TM, :]
    h = _rms(x, gpre_ref[...]).astype(BF16)

    chunk_cols = 256
    n_chunks = D_MODEL // chunk_cols

    def proj_chunk(j, c):
        c0 = j * D_MODEL + c * chunk_cols
        return jnp.dot(h, _unpack(win_ref, c0, c0 + chunk_cols), preferred_element_type=F32)

    def cols(c):
        return slice(c * chunk_cols, (c + 1) * chunk_cols)

    def row_cat(parts):
        return jnp.concatenate(parts, axis=1)


    lbl = lbl_ref[...]
    le = jnp.exp(lbl - jnp.max(lbl, axis=0, keepdims=True))
    lb = le[0:1, :] / jnp.sum(le, axis=0, keepdims=True)
    f_raw = [proj_chunk(4, c) for c in range(n_chunks)]
    ti = lax.broadcasted_iota(jnp.int32, (TM, TM), 0)
    si = lax.broadcasted_iota(jnp.int32, (TM, TM), 1)
    tri = jnp.where(ti >= si, 1.0, 0.0).astype(BF16)
    forget_c, b_c, qh_c = [], [], []
    for c in range(n_chunks):
        q_raw = proj_chunk(3, c)
        fg = lb[:, cols(c)] + (1.0 - lb[:, cols(c)]) * _sigmoid(f_raw[c])
        lf = jnp.log(fg)
        lf_hi = lf.astype(BF16)
        lf_lo = (lf - lf_hi.astype(F32)).astype(BF16)
        bc = jnp.dot(tri, lf_hi, preferred_element_type=F32) + jnp.dot(tri, lf_lo, preferred_element_type=F32)
        forget_c.append(fg)
        b_c.append(bc * LOG2E)
        qh_c.append(_silu(q_raw) * (HEAD_DIM ** -0.5))
    forget = row_cat(forget_c)
    kk = 1.0 - forget
    b = row_cat(b_c)
    qh = row_cat(qh_c)

    gv_c = [_gelu(proj_chunk(1, c)) for c in range(n_chunks)]
    vf = row_cat([proj_chunk(5, c) for c in range(n_chunks)])
    vv = vf.astype(BF16)
    za_early = [proj_chunk(2, 0), proj_chunk(2, 1)]
    mu = sum(jnp.sum(g, axis=-1, keepdims=True) for g in gv_c) * (1.0 / D_MODEL)
    vc_c = [g - mu for g in gv_c]
    var = sum(jnp.sum(v * v, axis=-1, keepdims=True) for v in vc_c) * (1.0 / D_MODEL)
    rstd = lax.rsqrt(var + EPS)

    r128 = lax.broadcasted_iota(jnp.int32, (GMLP_CHUNK, GMLP_CHUNK), 0)
    c128 = lax.broadcasted_iota(jnp.int32, (GMLP_CHUNK, GMLP_CHUNK), 1)
    gu_c, sp_c = [], []
    for g in range(GMLP_GROUPS):
        gu_c.append(_gelu(proj_chunk(0, g)))
        vn = (vc_c[g] * rstd * lng_ref[:, cols(g)] + lnb_ref[:, cols(g)]).astype(BF16)
        wg = jnp.where(r128 >= c128, wsp_ref[g], jnp.zeros((), BF16))
        bias = jnp.broadcast_to(bs_ref[:, g:g + 1], (GMLP_CHUNK, GMLP_GROUP_DIM))
        sp_c.append(jnp.concatenate(
            [jnp.dot(wg, vn[r * GMLP_CHUNK:(r + 1) * GMLP_CHUNK], preferred_element_type=F32) + bias
             for r in range(TM // GMLP_CHUNK)], axis=0))

    ya_parts = []

    def branch_a_chunk(c, za_raw):
        ya_in = (gu_c[c] * sp_c[c] * _silu(za_raw)).astype(BF16)
        w = pltpu.bitcast(wba_ref[c * chunk_cols // 2:(c + 1) * chunk_cols // 2, :], BF16)
        ya_parts.append(jnp.dot(ya_in, w, preferred_element_type=F32))

    za_raw = za_early
    branch_a_chunk(0, za_raw[0])
    za_raw.append(proj_chunk(2, 2))
    branch_a_chunk(1, za_raw[1])
    za_raw.append(proj_chunk(2, 3))
    branch_a_chunk(2, za_raw[2])

    def level_operand(hs):
        if hs >= SUBLANES:
            shp = (TM // (2 * hs), 2 * hs, D_MODEL)
            b3, q3, k3 = b.reshape(shp), qh.reshape(shp), kk.reshape(shp)
            r = b3[:, hs - 1:hs, :]
            xk = k3[:, :hs, :] * jnp.exp2(r - b3[:, :hs, :])
            xq = q3[:, hs:, :] * jnp.exp2(b3[:, hs:, :] - r)
            return xk, xq
        shp = (TM // SUBLANES, SUBLANES, D_MODEL)
        b8, q8, k8 = b.reshape(shp), qh.reshape(shp), kk.reshape(shp)
        row8 = lax.broadcasted_iota(jnp.int32, (1, SUBLANES, D_MODEL), 1)
        if hs == 4:
            r = jnp.broadcast_to(b8[:, 3:4, :], shp)
        else:
            r = jnp.where(row8 < 4, jnp.broadcast_to(b8[:, 1:2, :], shp), jnp.broadcast_to(b8[:, 5:6, :], shp))
        is_q = (row8 % (2 * hs)) >= hs
        d = (b8 - r) * jnp.where(is_q, 1.0, -1.0)
        x = jnp.where(is_q, q8, k8) * jnp.exp2(d)
        return x.reshape(TM, D_MODEL).astype(BF16)

    def as_rows(x3):
        return x3.reshape(TOP_HALF, D_MODEL).astype(BF16)

    xk_top, xq_top = (as_rows(x3) for x3 in level_operand(TOP_HALF))
    block_levels = []
    for hs in MXU_HALVES:
        if hs >= SUBLANES:
            xk, xq = level_operand(hs)
            x_all = jnp.concatenate([xk, xq], axis=1).reshape(TM, D_MODEL).astype(BF16)
            block_levels.append((hs, x_all, as_rows(xq)))
    row_levels = [(hs, level_operand(hs)) for hs in MXU_HALVES if hs < SUBLANES]

    b_last = b[TM - 1:TM, :]
    qs = (qh * jnp.exp2(b)).astype(BF16)
    ks = (kk * jnp.exp2(b_last - b)).astype(BF16)
    dl = jnp.exp2(b_last)

    rows = lax.broadcasted_iota(jnp.int32, (TM, D_MODEL), 0)
    w0 = qh * kk
    w1 = jnp.where((rows & 1) == 1, qh * pltpu.roll(kk, 1, 0) * forget, 0.0)

    lane = lax.broadcasted_iota(jnp.int32, (SUBLANES, LANES), 1)
    srow = lax.broadcasted_iota(jnp.int32, (SUBLANES, LANES), 0)
    mask_cache = {}

    def lane_range(off, n):
        key = (off, n)
        if key not in mask_cache:
            mask_cache[key] = (lane >= off) & (lane < off + n)
        return mask_cache[key]

    def diag_block(hs, off):
        key = (hs, off, 'd')
        if key not in mask_cache:
            c = lane - off
            mask_cache[key] = ((c >= 0) & (c < SUBLANES) & ((c ^ srow) < 2 * hs)
                               & ((srow & hs) != 0) & ((c & hs) == 0))
        return mask_cache[key]

    def diagonal(off, below):
        key = (off, below, 'g')
        if key not in mask_cache:
            m = lane == (srow + (off - below))
            mask_cache[key] = m & ((srow & 1) == 1) if below else m
        return mask_cache[key]

    pending = [('a', n_chunks - 1)] + [(j, c) for j in (6, 7, 8) for c in range(n_chunks)]
    chunks = {6: [], 7: [], 8: []}

    def filler():
        if pending:
            j, c = pending.pop(0)
            if j == 'a':
                branch_a_chunk(c, za_raw[c])
            else:
                chunks[j].append(proj_chunk(j, c))

    n_reg = TM // SUBLANES
    zero = jnp.zeros((SUBLANES, LANES), F32)
    hng = hng_ref[...]
    o_heads = []

    def head_scores(hd):
        sl = slice(hd * HEAD_DIM, (hd + 1) * HEAD_DIM)
        a_t = [[zero] * n_reg for _ in range(TM // LANES)]
        p_top = _dot_nt(xq_top[:, sl], xk_top[:, sl])
        for i in range(TOP_HALF // SUBLANES):
            a_t[0][TOP_HALF // SUBLANES + i] = p_top[SUBLANES * i:SUBLANES * (i + 1)]
        for hs, x_all, xqc in block_levels:
            pc = _dot_nt(xqc[:, sl], x_all[:, sl])
            for j in range(TM // (2 * hs)):
                half, off = divmod(2 * hs * j, LANES)
                m = lane_range(off, hs)
                for kp in range(hs // SUBLANES):
                    src = j * hs + SUBLANES * kp
                    idx = (2 * hs * j + hs) // SUBLANES + kp
                    piece = pc[src:src + SUBLANES, half * LANES:(half + 1) * LANES]
                    a_t[half][idx] = jnp.where(m, piece, a_t[half][idx])
        filler()
        for hs, xl in row_levels:
            xh = xl[:, sl]
            pf = _dot_nt(xh, xh)
            for i in range(n_reg):
                half, off = divmod(SUBLANES * i, LANES)
                piece = pf[SUBLANES * i:SUBLANES * (i + 1), half * LANES:(half + 1) * LANES]
                a_t[half][i] = jnp.where(diag_block(hs, off), piece, a_t[half][i])
        s0 = jnp.sum(w0[:, sl], axis=-1, keepdims=True)
        s1 = jnp.sum(w1[:, sl], axis=-1, keepdims=True)
        for i in range(n_reg):
            half, off = divmod(SUBLANES * i, LANES)
            rs = slice(SUBLANES * i, SUBLANES * (i + 1))
            below = jnp.where(diagonal(off, 1), jnp.broadcast_to(s1[rs], (SUBLANES, LANES)), a_t[half][i])
            a_t[half][i] = jnp.where(diagonal(off, 0), jnp.broadcast_to(s0[rs], (SUBLANES, LANES)), below)
        a = jnp.concatenate([jnp.concatenate(col, axis=0) for col in a_t], axis=1)
        return a.astype(BF16)

    def head_output(hd, a):
        sl = slice(hd * HEAD_DIM, (hd + 1) * HEAD_DIM)
        vh = vv[:, sl]
        st = st_ref[hd]
        o_h = jnp.dot(a, vh, preferred_element_type=F32) + _dot_nt(qs[:, sl], st.astype(BF16))
        st_ref[hd] = st * dl[:, sl] + _dot_tn(vh, ks[:, sl])
        o_heads.append(_rms(o_h, hng[:, sl]))
        if hd % 2:
            filler()

    a_next = head_scores(0)
    for hd in range(HEADS):
        a_cur = a_next
        if hd + 1 < HEADS:
            a_next = head_scores(hd + 1)
        head_output(hd, a_cur)
    o_all = jnp.concatenate(o_heads, axis=1)
    while pending:
        filler()
    szb = _silu(row_cat(chunks[6]))
    y_a = ya_parts[0] + ya_parts[1] + ya_parts[2] + ya_parts[3]
    yb_in = (o_all * szb).astype(BF16)

    halves = [slice(i * TOP_HALF, (i + 1) * TOP_HALF) for i in range(2)]
    y_b = [jnp.dot(yb_in[r], _unpack(wbb_ref), preferred_element_type=F32) for r in halves]
    ga = _sigmoid(jnp.concatenate(chunks[7], axis=1))
    gb = _sigmoid(jnp.concatenate(chunks[8], axis=1))
    mo = [jnp.dot((ga[r] * y_a[r] + gb[r] * y_b[i]).astype(BF16), _unpack(wout_ref), preferred_element_type=F32)
          for i, r in enumerate(halves)]
    x1 = [x[r] + _rms(mo[i], gpost_ref[...]) for i, r in enumerate(halves)]
    gl = [jnp.dot(x1[i].astype(BF16), _unpack(wpg_ref), preferred_element_type=F32) + bpg_ref[...] for i in range(2)]
    for i, r in enumerate(halves):
        o_ref[r0 + r.start:r0 + r.stop, :] = x1[i] + _rms(e_ple[r] * _sigmoid(gl[i]), gple_ref[...])


def _pack_body(w_ref, o_ref):
    o_ref[...] = pltpu.bitcast(w_ref[...].astype(BF16), jnp.uint32)


def _pack_rows(w):
    k, n = w.shape
    bk, bn = min(k, PACK_BLOCK_ROWS), min(n, PACK_BLOCK_COLS)
    assert k % bk == 0 and n % bn == 0
    return pl.pallas_call(
        _pack_body,
        out_shape=jax.ShapeDtypeStruct((k // 2, n), jnp.uint32),
        grid=(k // bk, n // bn),
        in_specs=[pl.BlockSpec((bk, bn), lambda i, j: (i, j))],
        out_specs=pl.BlockSpec((bk // 2, bn), lambda i, j: (i, j)),
        name="pack_weight_rows",
    )(w)


def _const_spec(shape):
    nd = len(shape)
    return pl.BlockSpec(shape, lambda b, t: (0,) * nd, pipeline_mode=pl.Buffered(1))


@jax.jit
def kernel(x, p, w_in, gmlp_ln_g, gmlp_ln_b, gmlp_w_s, gmlp_b_s, hgrn_lb_logits, hgrn_norm_g, w_branch_a, w_branch_b, w_out, g_pre, g_post, w_ple, w_ple_gate, b_ple_gate, g_ple):
    bsz, t, d = x.shape
    assert d == D_MODEL and t % (TM * TILES_PER_STEP) == 0 and p.shape[0] == 1
    row = lambda a: a.reshape(1, -1).astype(F32)

    consts = (
        _pack_rows(w_in[0]),
        row(gmlp_ln_g[0]), row(gmlp_ln_b[0]),
        gmlp_w_s[0].astype(BF16),
        gmlp_b_s[0].T.astype(F32),
        hgrn_lb_logits.astype(F32),
        row(hgrn_norm_g[0]),
        _pack_rows(w_branch_a[0]), _pack_rows(w_branch_b[0]), _pack_rows(w_out[0]),
        row(g_pre[0]), row(g_post[0]),
        _pack_rows(w_ple[0]), _pack_rows(w_ple_gate[0]),
        row(b_ple_gate[0]), row(g_ple[0]),
    )
    tile = lambda width: pl.BlockSpec((None, TM * TILES_PER_STEP, width), lambda b, t: (b, t, 0))
    return pl.pallas_call(
        _body,
        out_shape=jax.ShapeDtypeStruct(x.shape, x.dtype),
        grid=(bsz, t // (TM * TILES_PER_STEP)),
        in_specs=[tile(D_MODEL), tile(PLE_DIM)] + [_const_spec(c.shape) for c in consts],
        out_specs=tile(D_MODEL),
        scratch_shapes=[pltpu.VMEM((HEADS, HEAD_DIM, HEAD_DIM), F32)],
        compiler_params=pltpu.CompilerParams(
            dimension_semantics=("arbitrary", "arbitrary"),
            vmem_limit_bytes=VMEM_LIMIT_BYTES,
        ),
        name="hybrid_block",
    )(x, p[0], *consts)
```

```python
import jax
import jax.numpy as jnp
from jax import lax
from jax.experimental import pallas as pl
from jax.experimental.pallas import tpu as pltpu

F32 = jnp.float32
BF16 = jnp.bfloat16

D_MODEL = 1024
PLE_DIM = 256
EPS = 1e-6
LOG2E = 1.4426950408889634
GMLP_GROUPS = 4
GMLP_GROUP_DIM = 256
GMLP_CHUNK = 128
HEADS = 8
HEAD_DIM = 128
SUBLANES = 8
LANES = 128

TM = 256
TILES_PER_STEP = 2
TOP_HALF = TM // 2
MXU_HALVES = (64, 32, 16, 8, 4, 2)
PACK_BLOCK_ROWS = 512
PACK_BLOCK_COLS = 1024
VMEM_LIMIT_BYTES = 60000 * 1024


def _rms(y, g):
    return y * lax.rsqrt(jnp.mean(y * y, axis=-1, keepdims=True) + EPS) * g


def _gelu(y):
    return 0.5 * y * (1.0 + lax.erf(y * (2.0 ** -0.5)))


def _sigmoid(y):
    return jax.nn.sigmoid(y)


def _silu(y):
    return y * _sigmoid(y)


def _unpack(w_ref, c0=None, c1=None):
    w = w_ref[...] if c0 is None else w_ref[:, c0:c1]
    return pltpu.bitcast(w, BF16)


def _dot_nt(a, b):
    return lax.dot_general(a, b, (((1,), (1,)), ((), ())), preferred_element_type=F32)


def _dot_tn(a, b):
    return lax.dot_general(a, b, (((0,), (0,)), ((), ())), preferred_element_type=F32)


def _body(x_ref, p_ref, win_ref, lng_ref, lnb_ref, wsp_ref, bs_ref, lbl_ref, hng_ref,
          wba_ref, wbb_ref, wout_ref, gpre_ref, gpost_ref, wple_ref, wpg_ref, bpg_ref, gple_ref,
          o_ref, st_ref):
    @pl.when(pl.program_id(1) == 0)
    def _():
        st_ref[...] = jnp.zeros_like(st_ref)

    refs = (win_ref, lng_ref, lnb_ref, wsp_ref, bs_ref, lbl_ref, hng_ref,
            wba_ref, wbb_ref, wout_ref, gpre_ref, gpost_ref, wple_ref, wpg_ref, bpg_ref, gple_ref, st_ref)
    for sub in range(TILES_PER_STEP):
        _tile(x_ref, p_ref, o_ref, sub * TM, *refs)


def _tile(x_ref, p_ref, o_ref, r0, win_ref, lng_ref, lnb_ref, wsp_ref, bs_ref, lbl_ref, hng_ref,
          wba_ref, wbb_ref, wout_ref, gpre_ref, gpost_ref, wple_ref, wpg_ref, bpg_ref, gple_ref, st_ref):

    e_ple = jnp.dot(p_ref[r0:r0 + TM, :].astype(BF16), _unpack(wple_ref), preferred_element_type=F32)
    x = x_ref[r0:r0 + TM, :]
    h = _rms(x, gpre_ref[...]).astype(BF16)

    chunk_cols = 256
    n_chunks = D_MODEL // chunk_cols

    def proj_chunk(j, c):
        c0 = j * D_MODEL + c * chunk_cols
        return jnp.dot(h, _unpack(win_ref, c0, c0 + chunk_cols), preferred_element_type=F32)

    def cols(c):
        return slice(c * chunk_cols, (c + 1) * chunk_cols)

    def row_cat(parts):
        return jnp.concatenate(parts, axis=1)


    lbl = lbl_ref[...]
    le = jnp.exp(lbl - jnp.max(lbl, axis=0, keepdims=True))
    lb = le[0:1, :] / jnp.sum(le, axis=0, keepdims=True)
    f_raw = [proj_chunk(4, c) for c in range(n_chunks)]
    ti = lax.broadcasted_iota(jnp.int32, (TM, TM), 0)
    si = lax.broadcasted_iota(jnp.int32, (TM, TM), 1)
    tri = jnp.where(ti >= si, 1.0, 0.0).astype(BF16)
    forget_c, b_c, qh_c = [], [], []
    for c in range(n_chunks):
        q_raw = proj_chunk(3, c)
        fg = lb[:, cols(c)] + (1.0 - lb[:, cols(c)]) * _sigmoid(f_raw[c])
        lf = jnp.log(fg)
        lf_hi = lf.astype(BF16)
        lf_lo = (lf - lf_hi.astype(F32)).astype(BF16)
        bc = jnp.dot(tri, lf_hi, preferred_element_type=F32) + jnp.dot(tri, lf_lo, preferred_element_type=F32)
        forget_c.append(fg)
        b_c.append(bc * LOG2E)
        qh_c.append(_silu(q_raw) * (HEAD_DIM ** -0.5))
    forget = row_cat(forget_c)
    kk = 1.0 - forget
    b = row_cat(b_c)
    qh = row_cat(qh_c)

    gv_c = [_gelu(proj_chunk(1, c)) for c in range(n_chunks)]
    vf = row_cat([proj_chunk(5, c) for c in range(n_chunks)])
    vv = vf.astype(BF16)
    za_early = [proj_chunk(2, 0), proj_chunk(2, 1)]
    mu = sum(jnp.sum(g, axis=-1, keepdims=True) for g in gv_c) * (1.0 / D_MODEL)
    vc_c = [g - mu for g in gv_c]
    var = sum(jnp.sum(v * v, axis=-1, keepdims=True) for v in vc_c) * (1.0 / D_MODEL)
    rstd = lax.rsqrt(var + EPS)

    r128 = lax.broadcasted_iota(jnp.int32, (GMLP_CHUNK, GMLP_CHUNK), 0)
    c128 = lax.broadcasted_iota(jnp.int32, (GMLP_CHUNK, GMLP_CHUNK), 1)
    gu_c, sp_c = [], []
    for g in range(GMLP_GROUPS):
        gu_c.append(_gelu(proj_chunk(0, g)))
        vn = (vc_c[g] * rstd * lng_ref[:, cols(g)] + lnb_ref[:, cols(g)]).astype(BF16)
        wg = jnp.where(r128 >= c128, wsp_ref[g], jnp.zeros((), BF16))
        bias = jnp.broadcast_to(bs_ref[:, g:g + 1], (GMLP_CHUNK, GMLP_GROUP_DIM))
        sp_c.append(jnp.concatenate(
            [jnp.dot(wg, vn[r * GMLP_CHUNK:(r + 1) * GMLP_CHUNK], preferred_element_type=F32) + bias
             for r in range(TM // GMLP_CHUNK)], axis=0))

    ya_parts = []

    def branch_a_chunk(c, za_raw):
        ya_in = (gu_c[c] * sp_c[c] * _silu(za_raw)).astype(BF16)
        w = pltpu.bitcast(wba_ref[c * chunk_cols // 2:(c + 1) * chunk_cols // 2, :], BF16)
        ya_parts.append(jnp.dot(ya_in, w, preferred_element_type=F32))

    za_raw = za_early
    branch_a_chunk(0, za_raw[0])
    za_raw.append(proj_chunk(2, 2))
    branch_a_chunk(1, za_raw[1])
    za_raw.append(proj_chunk(2, 3))
    branch_a_chunk(2, za_raw[2])

    def level_operand(hs):
        if hs >= SUBLANES:
            shp = (TM // (2 * hs), 2 * hs, D_MODEL)
            b3, q3, k3 = b.reshape(shp), qh.reshape(shp), kk.reshape(shp)
            r = b3[:, hs - 1:hs, :]
            xk = k3[:, :hs, :] * jnp.exp2(r - b3[:, :hs, :])
            xq = q3[:, hs:, :] * jnp.exp2(b3[:, hs:, :] - r)
            return xk, xq
        shp = (TM // SUBLANES, SUBLANES, D_MODEL)
        b8, q8, k8 = b.reshape(shp), qh.reshape(shp), kk.reshape(shp)
        row8 = lax.broadcasted_iota(jnp.int32, (1, SUBLANES, D_MODEL), 1)
        if hs == 4:
            r = jnp.broadcast_to(b8[:, 3:4, :], shp)
        else:
            r = jnp.where(row8 < 4, jnp.broadcast_to(b8[:, 1:2, :], shp), jnp.broadcast_to(b8[:, 5:6, :], shp))
        is_q = (row8 % (2 * hs)) >= hs
        d = (b8 - r) * jnp.where(is_q, 1.0, -1.0)
        x = jnp.where(is_q, q8, k8) * jnp.exp2(d)
        return x.reshape(TM, D_MODEL).astype(BF16)

    def as_rows(x3):
        return x3.reshape(TOP_HALF, D_MODEL).astype(BF16)

    xk_top, xq_top = (as_rows(x3) for x3 in level_operand(TOP_HALF))
    block_levels = []
    for hs in MXU_HALVES:
        if hs >= SUBLANES:
            xk, xq = level_operand(hs)
            x_all = jnp.concatenate([xk, xq], axis=1).reshape(TM, D_MODEL).astype(BF16)
            block_levels.append((hs, x_all, as_rows(xq)))
    row_levels = [(hs, level_operand(hs)) for hs in MXU_HALVES if hs < SUBLANES]

    b_last = b[TM - 1:TM, :]
    qs = (qh * jnp.exp2(b)).astype(BF16)
    ks = (kk * jnp.exp2(b_last - b)).astype(BF16)
    dl = jnp.exp2(b_last)

    rows = lax.broadcasted_iota(jnp.int32, (TM, D_MODEL), 0)
    w0 = qh * kk
    w1 = jnp.where((rows & 1) == 1, qh * pltpu.roll(kk, 1, 0) * forget, 0.0)

    lane = lax.broadcasted_iota(jnp.int32, (SUBLANES, LANES), 1)
    srow = lax.broadcasted_iota(jnp.int32, (SUBLANES, LANES), 0)
    srow3 = lax.broadcasted_iota(jnp.int32, (1, SUBLANES, HEAD_DIM), 1)
    mask_cache = {}

    def lane_range(off, n):
        key = (off, n)
        if key not in mask_cache:
            mask_cache[key] = (lane >= off) & (lane < off + n)
        return mask_cache[key]

    def diag_block(hs, off):
        key = (hs, off, 'd')
        if key not in mask_cache:
            c = lane - off
            mask_cache[key] = ((c >= 0) & (c < SUBLANES) & ((c ^ srow) < 2 * hs)
                               & ((srow & hs) != 0) & ((c & hs) == 0))
        return mask_cache[key]

    def diagonal(off, below):
        key = (off, below, 'g')
        if key not in mask_cache:
            m = lane == (srow + (off - below))
            mask_cache[key] = m & ((srow & 1) == 1) if below else m
        return mask_cache[key]

    pending = [('a', n_chunks - 1)] + [(j, c) for j in (6, 7, 8) for c in range(n_chunks)]
    chunks = {6: [], 7: [], 8: []}

    def filler():
        if pending:
            j, c = pending.pop(0)
            if j == 'a':
                branch_a_chunk(c, za_raw[c])
            else:
                chunks[j].append(proj_chunk(j, c))

    n_reg = TM // SUBLANES
    zero = jnp.zeros((SUBLANES, LANES), F32)
    hng = hng_ref[...]
    o_heads = []

    def head_scores(hd):
        sl = slice(hd * HEAD_DIM, (hd + 1) * HEAD_DIM)
        a_t = [[zero] * n_reg for _ in range(TM // LANES)]
        p_top = _dot_nt(xq_top[:, sl], xk_top[:, sl])
        for i in range(TOP_HALF // SUBLANES):
            a_t[0][TOP_HALF // SUBLANES + i] = p_top[SUBLANES * i:SUBLANES * (i + 1)]
        for hs, x_all, xqc in block_levels:
            pc = _dot_nt(xqc[:, sl], x_all[:, sl])
            for j in range(TM // (2 * hs)):
                half, off = divmod(2 * hs * j, LANES)
                m = lane_range(off, hs)
                for kp in range(hs // SUBLANES):
                    src = j * hs + SUBLANES * kp
                    idx = (2 * hs * j + hs) // SUBLANES + kp
                    piece = pc[src:src + SUBLANES, half * LANES:(half + 1) * LANES]
                    a_t[half][idx] = jnp.where(m, piece, a_t[half][idx])
        filler()
        for hs, xl in row_levels:
            xh = xl[:, sl]
            if hs == 4:
                x3 = xh.astype(F32).reshape(TM // (2 * SUBLANES), 2, SUBLANES, HEAD_DIM)
                lo = pltpu.roll(x3[:, 0], 4, 1)
                xq4 = jnp.where(srow3 < 4, lo, x3[:, 1]).reshape(TM // 2, HEAD_DIM).astype(BF16)
                pq = _dot_nt(xq4, xh)
                for i in range(n_reg):
                    half, off = divmod(SUBLANES * i, LANES)
                    k = i // 2
                    piece = pq[SUBLANES * k:SUBLANES * (k + 1), half * LANES:(half + 1) * LANES]
                    if i % 2 == 0:
                        piece = pltpu.roll(piece, 4, 0)
                    a_t[half][i] = jnp.where(diag_block(hs, off), piece, a_t[half][i])
                continue
            pf = _dot_nt(xh, xh)
            for i in range(n_reg):
                half, off = divmod(SUBLANES * i, LANES)
                piece = pf[SUBLANES * i:SUBLANES * (i + 1), half * LANES:(half + 1) * LANES]
                a_t[half][i] = jnp.where(diag_block(hs, off), piece, a_t[half][i])
        s0 = jnp.sum(w0[:, sl], axis=-1, keepdims=True)
        s1 = jnp.sum(w1[:, sl], axis=-1, keepdims=True)
        for i in range(n_reg):
            half, off = divmod(SUBLANES * i, LANES)
            rs = slice(SUBLANES * i, SUBLANES * (i + 1))
            below = jnp.where(diagonal(off, 1), jnp.broadcast_to(s1[rs], (SUBLANES, LANES)), a_t[half][i])
            a_t[half][i] = jnp.where(diagonal(off, 0), jnp.broadcast_to(s0[rs], (SUBLANES, LANES)), below)
        a = jnp.concatenate([jnp.concatenate(col, axis=0) for col in a_t], axis=1)
        return a.astype(BF16)

    def head_output(hd, a):
        sl = slice(hd * HEAD_DIM, (hd + 1) * HEAD_DIM)
        vh = vv[:, sl]
        st = st_ref[hd]
        o_h = jnp.dot(a, vh, preferred_element_type=F32) + _dot_nt(qs[:, sl], st.astype(BF16))
        st_ref[hd] = st * dl[:, sl] + _dot_tn(vh, ks[:, sl])
        o_heads.append(_rms(o_h, hng[:, sl]))
        if hd % 2:
            filler()

    a_next = head_scores(0)
    for hd in range(HEADS):
        a_cur = a_next
        if hd + 1 < HEADS:
            a_next = head_scores(hd + 1)
        head_output(hd, a_cur)
    o_all = jnp.concatenate(o_heads, axis=1)
    while pending:
        filler()
    szb = _silu(row_cat(chunks[6]))
    y_a = ya_parts[0] + ya_parts[1] + ya_parts[2] + ya_parts[3]
    yb_in = (o_all * szb).astype(BF16)

    halves = [slice(i * TOP_HALF, (i + 1) * TOP_HALF) for i in range(2)]
    y_b = [jnp.dot(yb_in[r], _unpack(wbb_ref), preferred_element_type=F32) for r in halves]
    ga = _sigmoid(jnp.concatenate(chunks[7], axis=1))
    gb = _sigmoid(jnp.concatenate(chunks[8], axis=1))
    mo = [jnp.dot((ga[r] * y_a[r] + gb[r] * y_b[i]).astype(BF16), _unpack(wout_ref), preferred_element_type=F32)
          for i, r in enumerate(halves)]
    x1 = [x[r] + _rms(mo[i], gpost_ref[...]) for i, r in enumerate(halves)]
    gl = [jnp.dot(x1[i].astype(BF16), _unpack(wpg_ref), preferred_element_type=F32) + bpg_ref[...] for i in range(2)]
    for i, r in enumerate(halves):
        o_ref[r0 + r.start:r0 + r.stop, :] = x1[i] + _rms(e_ple[r] * _sigmoid(gl[i]), gple_ref[...])


def _pack_body(w_ref, o_ref):
    o_ref[...] = pltpu.bitcast(w_ref[...].astype(BF16), jnp.uint32)


def _pack_rows(w):
    k, n = w.shape
    bk, bn = min(k, PACK_BLOCK_ROWS), min(n, PACK_BLOCK_COLS)
    assert k % bk == 0 and n % bn == 0
    return pl.pallas_call(
        _pack_body,
        out_shape=jax.ShapeDtypeStruct((k // 2, n), jnp.uint32),
        grid=(k // bk, n // bn),
        in_specs=[pl.BlockSpec((bk, bn), lambda i, j: (i, j))],
        out_specs=pl.BlockSpec((bk // 2, bn), lambda i, j: (i, j)),
        name="pack_weight_rows",
    )(w)


def _const_spec(shape):
    nd = len(shape)
    return pl.BlockSpec(shape, lambda b, t: (0,) * nd, pipeline_mode=pl.Buffered(1))


@jax.jit
def kernel(x, p, w_in, gmlp_ln_g, gmlp_ln_b, gmlp_w_s, gmlp_b_s, hgrn_lb_logits, hgrn_norm_g, w_branch_a, w_branch_b, w_out, g_pre, g_post, w_ple, w_ple_gate, b_ple_gate, g_ple):
    bsz, t, d = x.shape
    assert d == D_MODEL and t % (TM * TILES_PER_STEP) == 0 and p.shape[0] == 1
    row = lambda a: a.reshape(1, -1).astype(F32)

    consts = (
        _pack_rows(w_in[0]),
        row(gmlp_ln_g[0]), row(gmlp_ln_b[0]),
        gmlp_w_s[0].astype(BF16),
        gmlp_b_s[0].T.astype(F32),
        hgrn_lb_logits.astype(F32),
        row(hgrn_norm_g[0]),
        _pack_rows(w_branch_a[0]), _pack_rows(w_branch_b[0]), _pack_rows(w_out[0]),
        row(g_pre[0]), row(g_post[0]),
        _pack_rows(w_ple[0]), _pack_rows(w_ple_gate[0]),
        row(b_ple_gate[0]), row(g_ple[0]),
    )
    tile = lambda width: pl.BlockSpec((None, TM * TILES_PER_STEP, width), lambda b, t: (b, t, 0))
    return pl.pallas_call(
        _body,
        out_shape=jax.ShapeDtypeStruct(x.shape, x.dtype),
        grid=(bsz, t // (TM * TILES_PER_STEP)),
        in_specs=[tile(D_MODEL), tile(PLE_DIM)] + [_const_spec(c.shape) for c in consts],
        out_specs=tile(D_MODEL),
        scratch_shapes=[pltpu.VMEM((HEADS, HEAD_DIM, HEAD_DIM), F32)],
        compiler_params=pltpu.CompilerParams(
            dimension_semantics=("arbitrary", "arbitrary"),
            vmem_limit_bytes=VMEM_LIMIT_BYTES,
        ),
        name="hybrid_block",
    )(x, p[0], *consts)
```

```python
import jax
import jax.numpy as jnp
from jax import lax
from jax.experimental import pallas as pl
from jax.experimental.pallas import tpu as pltpu

F32 = jnp.float32
BF16 = jnp.bfloat16

D_MODEL = 1024
PLE_DIM = 256
EPS = 1e-6
LOG2E = 1.4426950408889634
GMLP_GROUPS = 4
GMLP_GROUP_DIM = 256
GMLP_CHUNK = 128
HEADS = 8
HEAD_DIM = 128
SUBLANES = 8
LANES = 128

TM = 256
TILES_PER_STEP = 2
TOP_HALF = TM // 2
MXU_HALVES = (64, 32, 16, 8, 4, 2)
PACK_BLOCK_ROWS = 512
PACK_BLOCK_COLS = 1024
VMEM_LIMIT_BYTES = 60000 * 1024


def _rms(y, g):
    return y * lax.rsqrt(jnp.mean(y * y, axis=-1, keepdims=True) + EPS) * g


def _gelu(y):
    return 0.5 * y * (1.0 + lax.erf(y * (2.0 ** -0.5)))


def _sigmoid(y):
    return jax.nn.sigmoid(y)


def _silu(y):
    return y * _sigmoid(y)


def _unpack(w_ref, c0=None, c1=None):
    w = w_ref[...] if c0 is None else w_ref[:, c0:c1]
    return pltpu.bitcast(w, BF16)


def _dot_nt(a, b):
    return lax.dot_general(a, b, (((1,), (1,)), ((), ())), preferred_element_type=F32)


def _dot_tn(a, b):
    return lax.dot_general(a, b, (((0,), (0,)), ((), ())), preferred_element_type=F32)


def _body(x_ref, p_ref, win_ref, lng_ref, lnb_ref, wsp_ref, bs_ref, lbl_ref, hng_ref,
          wba_ref, wbb_ref, wout_ref, gpre_ref, gpost_ref, wple_ref, wpg_ref, bpg_ref, gple_ref,
          o_ref, st_ref):
    @pl.when(pl.program_id(1) == 0)
    def _():
        st_ref[...] = jnp.zeros_like(st_ref)

    refs = (win_ref, lng_ref, lnb_ref, wsp_ref, bs_ref, lbl_ref, hng_ref,
            wba_ref, wbb_ref, wout_ref, gpre_ref, gpost_ref, wple_ref, wpg_ref, bpg_ref, gple_ref, st_ref)
    for sub in range(TILES_PER_STEP):
        _tile(x_ref, p_ref, o_ref, sub * TM, *refs)


def _tile(x_ref, p_ref, o_ref, r0, win_ref, lng_ref, lnb_ref, wsp_ref, bs_ref, lbl_ref, hng_ref,
          wba_ref, wbb_ref, wout_ref, gpre_ref, gpost_ref, wple_ref, wpg_ref, bpg_ref, gple_ref, st_ref):

    e_ple = jnp.dot(p_ref[r0:r0 + TM, :].astype(BF16), _unpack(wple_ref), preferred_element_type=F32)
    x = x_ref[r0:r0 + TM, :]
    h = _rms(x, gpre_ref[...]).astype(BF16)

    chunk_cols = 256
    n_chunks = D_MODEL // chunk_cols

    def proj_chunk(j, c):
        c0 = j * D_MODEL + c * chunk_cols
        return jnp.dot(h, _unpack(win_ref, c0, c0 + chunk_cols), preferred_element_type=F32)

    def cols(c):
        return slice(c * chunk_cols, (c + 1) * chunk_cols)

    def row_cat(parts):
        return jnp.concatenate(parts, axis=1)


    lbl = lbl_ref[...]
    le = jnp.exp(lbl - jnp.max(lbl, axis=0, keepdims=True))
    lb = le[0:1, :] / jnp.sum(le, axis=0, keepdims=True)
    f_raw = [proj_chunk(4, c) for c in range(n_chunks)]
    ti = lax.broadcasted_iota(jnp.int32, (TM, TM), 0)
    si = lax.broadcasted_iota(jnp.int32, (TM, TM), 1)
    tri = jnp.where(ti >= si, 1.0, 0.0).astype(BF16)
    forget_c, b_c, qh_c = [], [], []
    for c in range(n_chunks):
        q_raw = proj_chunk(3, c)
        fg = lb[:, cols(c)] + (1.0 - lb[:, cols(c)]) * _sigmoid(f_raw[c])
        lf = jnp.log(fg)
        lf_hi = lf.astype(BF16)
        lf_lo = (lf - lf_hi.astype(F32)).astype(BF16)
        bc = jnp.dot(tri, lf_hi, preferred_element_type=F32) + jnp.dot(tri, lf_lo, preferred_element_type=F32)
        forget_c.append(fg)
        b_c.append(bc * LOG2E)
        qh_c.append(_silu(q_raw) * (HEAD_DIM ** -0.5))
    forget = row_cat(forget_c)
    kk = 1.0 - forget
    b = row_cat(b_c)
    qh = row_cat(qh_c)

    gv_c = [_gelu(proj_chunk(1, c)) for c in range(n_chunks)]
    vf = row_cat([proj_chunk(5, c) for c in range(n_chunks)])
    vv = vf.astype(BF16)
    za_early = [proj_chunk(2, 0), proj_chunk(2, 1)]
    mu = sum(jnp.sum(g, axis=-1, keepdims=True) for g in gv_c) * (1.0 / D_MODEL)
    vc_c = [g - mu for g in gv_c]
    var = sum(jnp.sum(v * v, axis=-1, keepdims=True) for v in vc_c) * (1.0 / D_MODEL)
    rstd = lax.rsqrt(var + EPS)

    r128 = lax.broadcasted_iota(jnp.int32, (GMLP_CHUNK, GMLP_CHUNK), 0)
    c128 = lax.broadcasted_iota(jnp.int32, (GMLP_CHUNK, GMLP_CHUNK), 1)
    gu_c, sp_c = [], []
    for g in range(GMLP_GROUPS):
        gu_c.append(_gelu(proj_chunk(0, g)))
        vn = (vc_c[g] * rstd * lng_ref[:, cols(g)] + lnb_ref[:, cols(g)]).astype(BF16)
        wg = jnp.where(r128 >= c128, wsp_ref[g], jnp.zeros((), BF16))
        bias = jnp.broadcast_to(bs_ref[:, g:g + 1], (GMLP_CHUNK, GMLP_GROUP_DIM))
        sp_c.append(jnp.concatenate(
            [jnp.dot(wg, vn[r * GMLP_CHUNK:(r + 1) * GMLP_CHUNK], preferred_element_type=F32) + bias
             for r in range(TM // GMLP_CHUNK)], axis=0))

    ya_parts = []

    def branch_a_chunk(c, za_raw):
        ya_in = (gu_c[c] * sp_c[c] * _silu(za_raw)).astype(BF16)
        w = pltpu.bitcast(wba_ref[c * chunk_cols // 2:(c + 1) * chunk_cols // 2, :], BF16)
        ya_parts.append(jnp.dot(ya_in, w, preferred_element_type=F32))

    za_raw = za_early
    branch_a_chunk(0, za_raw[0])
    za_raw.append(proj_chunk(2, 2))
    branch_a_chunk(1, za_raw[1])
    za_raw.append(proj_chunk(2, 3))
    branch_a_chunk(2, za_raw[2])

    def level_operand(hs):
        if hs >= SUBLANES:
            shp = (TM // (2 * hs), 2 * hs, D_MODEL)
            b3, q3, k3 = b.reshape(shp), qh.reshape(shp), kk.reshape(shp)
            r = b3[:, hs - 1:hs, :]
            xk = k3[:, :hs, :] * jnp.exp2(r - b3[:, :hs, :])
            xq = q3[:, hs:, :] * jnp.exp2(b3[:, hs:, :] - r)
            return xk, xq
        shp = (TM // SUBLANES, SUBLANES, D_MODEL)
        b8, q8, k8 = b.reshape(shp), qh.reshape(shp), kk.reshape(shp)
        row8 = lax.broadcasted_iota(jnp.int32, (1, SUBLANES, D_MODEL), 1)
        if hs == 4:
            r = jnp.broadcast_to(b8[:, 3:4, :], shp)
        else:
            r = jnp.where(row8 < 4, jnp.broadcast_to(b8[:, 1:2, :], shp), jnp.broadcast_to(b8[:, 5:6, :], shp))
        is_q = (row8 % (2 * hs)) >= hs
        d = (b8 - r) * jnp.where(is_q, 1.0, -1.0)
        x = jnp.where(is_q, q8, k8) * jnp.exp2(d)
        return x.reshape(TM, D_MODEL).astype(BF16)

    def as_rows(x3):
        return x3.reshape(TOP_HALF, D_MODEL).astype(BF16)

    xk_top, xq_top = (as_rows(x3) for x3 in level_operand(TOP_HALF))
    block_levels = []
    for hs in MXU_HALVES:
        if hs >= SUBLANES:
            xk, xq = level_operand(hs)
            x_all = jnp.concatenate([xk, xq], axis=1).reshape(TM, D_MODEL).astype(BF16)
            block_levels.append((hs, x_all, as_rows(xq)))
    row_levels = [(hs, level_operand(hs)) for hs in MXU_HALVES if hs < SUBLANES]

    b_last = b[TM - 1:TM, :]
    qs = (qh * jnp.exp2(b)).astype(BF16)
    ks = (kk * jnp.exp2(b_last - b)).astype(BF16)
    dl = jnp.exp2(b_last)

    rows = lax.broadcasted_iota(jnp.int32, (TM, D_MODEL), 0)
    w0 = qh * kk
    w1 = jnp.where((rows & 1) == 1, qh * pltpu.roll(kk, 1, 0) * forget, 0.0)

    lane = lax.broadcasted_iota(jnp.int32, (SUBLANES, LANES), 1)
    srow = lax.broadcasted_iota(jnp.int32, (SUBLANES, LANES), 0)
    srow3 = lax.broadcasted_iota(jnp.int32, (1, SUBLANES, HEAD_DIM), 1)
    mask_cache = {}

    def lane_range(off, n):
        key = (off, n)
        if key not in mask_cache:
            mask_cache[key] = (lane >= off) & (lane < off + n)
        return mask_cache[key]

    def diag_block(hs, off):
        key = (hs, off, 'd')
        if key not in mask_cache:
            c = lane - off
            mask_cache[key] = ((c >= 0) & (c < SUBLANES) & ((c ^ srow) < 2 * hs)
                               & ((srow & hs) != 0) & ((c & hs) == 0))
        return mask_cache[key]

    def diagonal(off, below):
        key = (off, below, 'g')
        if key not in mask_cache:
            m = lane == (srow + (off - below))
            mask_cache[key] = m & ((srow & 1) == 1) if below else m
        return mask_cache[key]

    pending = [('a', n_chunks - 1)] + [(j, c) for j in (6, 7, 8) for c in range(n_chunks)]
    chunks = {6: [], 7: [], 8: []}

    def filler():
        if pending:
            j, c = pending.pop(0)
            if j == 'a':
                branch_a_chunk(c, za_raw[c])
            else:
                chunks[j].append(proj_chunk(j, c))

    n_reg = TM // SUBLANES
    zero = jnp.zeros((SUBLANES, LANES), F32)
    hng = hng_ref[...]
    o_heads = []

    def head_scores(hd):
        sl = slice(hd * HEAD_DIM, (hd + 1) * HEAD_DIM)
        a_t = [[zero] * n_reg for _ in range(TM // LANES)]
        p_top = _dot_nt(xq_top[:, sl], xk_top[:, sl])
        for i in range(TOP_HALF // SUBLANES):
            a_t[0][TOP_HALF // SUBLANES + i] = p_top[SUBLANES * i:SUBLANES * (i + 1)]
        for hs, x_all, xqc in block_levels:
            pc = _dot_nt(xqc[:, sl], x_all[:, sl])
            for j in range(TM // (2 * hs)):
                half, off = divmod(2 * hs * j, LANES)
                m = lane_range(off, hs)
                for kp in range(hs // SUBLANES):
                    src = j * hs + SUBLANES * kp
                    idx = (2 * hs * j + hs) // SUBLANES + kp
                    piece = pc[src:src + SUBLANES, half * LANES:(half + 1) * LANES]
                    a_t[half][idx] = jnp.where(m, piece, a_t[half][idx])
        filler()
        for hs, xl in row_levels:
            xh = xl[:, sl]
            if hs == 4:
                x3 = xh.astype(F32).reshape(TM // (2 * SUBLANES), 2, SUBLANES, HEAD_DIM)
                lo = pltpu.roll(x3[:, 0], 4, 1)
                xq4 = jnp.where(srow3 < 4, lo, x3[:, 1]).reshape(TM // 2, HEAD_DIM).astype(BF16)
                pq = _dot_nt(xq4, xh)
                for i in range(n_reg):
                    half, off = divmod(SUBLANES * i, LANES)
                    k = i // 2
                    piece = pq[SUBLANES * k:SUBLANES * (k + 1), half * LANES:(half + 1) * LANES]
                    if i % 2 == 0:
                        piece = pltpu.roll(piece, 4, 0)
                    a_t[half][i] = jnp.where(diag_block(hs, off), piece, a_t[half][i])
                continue
            x3 = xh.astype(F32).reshape(TM // (2 * SUBLANES), 2, SUBLANES, HEAD_DIM)
            ra, rb = x3[:, 0], x3[:, 1]
            lo = jnp.where(srow3 < 2, pltpu.roll(ra, 6, 1), pltpu.roll(ra, 4, 1))
            hi = jnp.where(srow3 < 6, pltpu.roll(rb, 2, 1), rb)
            xq2 = jnp.where(srow3 < 4, lo, hi).reshape(TM // 2, HEAD_DIM).astype(BF16)
            pq = _dot_nt(xq2, xh)
            for i in range(n_reg):
                half, off = divmod(SUBLANES * i, LANES)
                k = i // 2
                src = pq[SUBLANES * k:SUBLANES * (k + 1), half * LANES:(half + 1) * LANES]
                if i % 2 == 0:
                    piece = jnp.where(srow < 4, pltpu.roll(src, 2, 0), pltpu.roll(src, 4, 0))
                else:
                    piece = jnp.where(srow < 4, pltpu.roll(src, 6, 0), src)
                a_t[half][i] = jnp.where(diag_block(hs, off), piece, a_t[half][i])
        s0 = jnp.sum(w0[:, sl], axis=-1, keepdims=True)
        s1 = jnp.sum(w1[:, sl], axis=-1, keepdims=True)
        for i in range(n_reg):
            half, off = divmod(SUBLANES * i, LANES)
            rs = slice(SUBLANES * i, SUBLANES * (i + 1))
            below = jnp.where(diagonal(off, 1), jnp.broadcast_to(s1[rs], (SUBLANES, LANES)), a_t[half][i])
            a_t[half][i] = jnp.where(diagonal(off, 0), jnp.broadcast_to(s0[rs], (SUBLANES, LANES)), below)
        a = jnp.concatenate([jnp.concatenate(col, axis=0) for col in a_t], axis=1)
        return a.astype(BF16)

    def head_output(hd, a):
        sl = slice(hd * HEAD_DIM, (hd + 1) * HEAD_DIM)
        vh = vv[:, sl]
        st = st_ref[hd]
        o_h = jnp.dot(a, vh, preferred_element_type=F32) + _dot_nt(qs[:, sl], st.astype(BF16))
        st_ref[hd] = st * dl[:, sl] + _dot_tn(vh, ks[:, sl])
        o_heads.append(_rms(o_h, hng[:, sl]))
        if hd % 2:
            filler()

    a_next = head_scores(0)
    for hd in range(HEADS):
        a_cur = a_next
        if hd + 1 < HEADS:
            a_next = head_scores(hd + 1)
        head_output(hd, a_cur)
    o_all = jnp.concatenate(o_heads, axis=1)
    while pending:
        filler()
    szb = _silu(row_cat(chunks[6]))
    y_a = ya_parts[0] + ya_parts[1] + ya_parts[2] + ya_parts[3]
    yb_in = (o_all * szb).astype(BF16)

    halves = [slice(i * TOP_HALF, (i + 1) * TOP_HALF) for i in range(2)]
    y_b = [jnp.dot(yb_in[r], _unpack(wbb_ref), preferred_element_type=F32) for r in halves]
    ga = _sigmoid(jnp.concatenate(chunks[7], axis=1))
    gb = _sigmoid(jnp.concatenate(chunks[8], axis=1))
    mo = [jnp.dot((ga[r] * y_a[r] + gb[r] * y_b[i]).astype(BF16), _unpack(wout_ref), preferred_element_type=F32)
          for i, r in enumerate(halves)]
    x1 = [x[r] + _rms(mo[i], gpost_ref[...]) for i, r in enumerate(halves)]
    gl = [jnp.dot(x1[i].astype(BF16), _unpack(wpg_ref), preferred_element_type=F32) + bpg_ref[...] for i in range(2)]
    for i, r in enumerate(halves):
        o_ref[r0 + r.start:r0 + r.stop, :] = x1[i] + _rms(e_ple[r] * _sigmoid(gl[i]), gple_ref[...])


def _pack_body(w_ref, o_ref):
    o_ref[...] = pltpu.bitcast(w_ref[...].astype(BF16), jnp.uint32)


def _pack_rows(w):
    k, n = w.shape
    bk, bn = min(k, PACK_BLOCK_ROWS), min(n, PACK_BLOCK_COLS)
    assert k % bk == 0 and n % bn == 0
    return pl.pallas_call(
        _pack_body,
        out_shape=jax.ShapeDtypeStruct((k // 2, n), jnp.uint32),
        grid=(k // bk, n // bn),
        in_specs=[pl.BlockSpec((bk, bn), lambda i, j: (i, j))],
        out_specs=pl.BlockSpec((bk // 2, bn), lambda i, j: (i, j)),
        name="pack_weight_rows",
    )(w)


def _const_spec(shape):
    nd = len(shape)
    return pl.BlockSpec(shape, lambda b, t: (0,) * nd, pipeline_mode=pl.Buffered(1))


@jax.jit
def kernel(x, p, w_in, gmlp_ln_g, gmlp_ln_b, gmlp_w_s, gmlp_b_s, hgrn_lb_logits, hgrn_norm_g, w_branch_a, w_branch_b, w_out, g_pre, g_post, w_ple, w_ple_gate, b_ple_gate, g_ple):
    bsz, t, d = x.shape
    assert d == D_MODEL and t % (TM * TILES_PER_STEP) == 0 and p.shape[0] == 1
    row = lambda a: a.reshape(1, -1).astype(F32)

    consts = (
        _pack_rows(w_in[0]),
        row(gmlp_ln_g[0]), row(gmlp_ln_b[0]),
        gmlp_w_s[0].astype(BF16),
        gmlp_b_s[0].T.astype(F32),
        hgrn_lb_logits.astype(F32),
        row(hgrn_norm_g[0]),
        _pack_rows(w_branch_a[0]), _pack_rows(w_branch_b[0]), _pack_rows(w_out[0]),
        row(g_pre[0]), row(g_post[0]),
        _pack_rows(w_ple[0]), _pack_rows(w_ple_gate[0]),
        row(b_ple_gate[0]), row(g_ple[0]),
    )
    tile = lambda width: pl.BlockSpec((None, TM * TILES_PER_STEP, width), lambda b, t: (b, t, 0))
    return pl.pallas_call(
        _body,
        out_shape=jax.ShapeDtypeStruct(x.shape, x.dtype),
        grid=(bsz, t // (TM * TILES_PER_STEP)),
        in_specs=[tile(D_MODEL), tile(PLE_DIM)] + [_const_spec(c.shape) for c in consts],
        out_specs=tile(D_MODEL),
        scratch_shapes=[pltpu.VMEM((HEADS, HEAD_DIM, HEAD_DIM), F32)],
        compiler_params=pltpu.CompilerParams(
            dimension_semantics=("arbitrary", "arbitrary"),
            vmem_limit_bytes=VMEM_LIMIT_BYTES,
        ),
        name="hybrid_block",
    )(x, p[0], *consts)
```

```python
import jax
import jax.numpy as jnp
from jax import lax
from jax.experimental import pallas as pl
from jax.experimental.pallas import tpu as pltpu

F32 = jnp.float32
BF16 = jnp.bfloat16

D_MODEL = 1024
PLE_DIM = 256
EPS = 1e-6
LOG2E = 1.4426950408889634
GMLP_GROUPS = 4
GMLP_GROUP_DIM = 256
GMLP_CHUNK = 128
HEADS = 8
HEAD_DIM = 128
SUBLANES = 8
LANES = 128

TM = 256
TILES_PER_STEP = 2
TOP_HALF = TM // 2
MXU_HALVES = (64, 32, 16, 8, 4, 2)
PACK_BLOCK_ROWS = 512
PACK_BLOCK_COLS = 1024
VMEM_LIMIT_BYTES = 60000 * 1024


def _rms(y, g):
    return y * lax.rsqrt(jnp.mean(y * y, axis=-1, keepdims=True) + EPS) * g


def _gelu(y):
    return 0.5 * y * (1.0 + lax.erf(y * (2.0 ** -0.5)))


def _sigmoid(y):
    return jax.nn.sigmoid(y)


def _silu(y):
    return y * _sigmoid(y)


def _unpack(w_ref, c0=None, c1=None):
    w = w_ref[...] if c0 is None else w_ref[:, c0:c1]
    return pltpu.bitcast(w, BF16)


def _dot_nt(a, b):
    return lax.dot_general(a, b, (((1,), (1,)), ((), ())), preferred_element_type=F32)


def _dot_tn(a, b):
    return lax.dot_general(a, b, (((0,), (0,)), ((), ())), preferred_element_type=F32)


def _body(x_ref, p_ref, win_ref, lng_ref, lnb_ref, wsp_ref, bs_ref, lbl_ref, hng_ref,
          wba_ref, wbb_ref, wout_ref, gpre_ref, gpost_ref, wple_ref, wpg_ref, bpg_ref, gple_ref,
          o_ref, st_ref):
    @pl.when(pl.program_id(1) == 0)
    def _():
        st_ref[...] = jnp.zeros_like(st_ref)

    refs = (win_ref, lng_ref, lnb_ref, wsp_ref, bs_ref, lbl_ref, hng_ref,
            wba_ref, wbb_ref, wout_ref, gpre_ref, gpost_ref, wple_ref, wpg_ref, bpg_ref, gple_ref, st_ref)
    for sub in range(TILES_PER_STEP):
        _tile(x_ref, p_ref, o_ref, sub * TM, *refs)


def _tile(x_ref, p_ref, o_ref, r0, win_ref, lng_ref, lnb_ref, wsp_ref, bs_ref, lbl_ref, hng_ref,
          wba_ref, wbb_ref, wout_ref, gpre_ref, gpost_ref, wple_ref, wpg_ref, bpg_ref, gple_ref, st_ref):

    e_ple = jnp.dot(p_ref[r0:r0 + TM, :].astype(BF16), _unpack(wple_ref), preferred_element_type=F32)
    x = x_ref[r0:r0 + TM, :]
    h = _rms(x, gpre_ref[...]).astype(BF16)

    chunk_cols = 256
    n_chunks = D_MODEL // chunk_cols

    def proj_chunk(j, c):
        c0 = j * D_MODEL + c * chunk_cols
        return jnp.dot(h, _unpack(win_ref, c0, c0 + chunk_cols), preferred_element_type=F32)

    def cols(c):
        return slice(c * chunk_cols, (c + 1) * chunk_cols)

    def row_cat(parts):
        return jnp.concatenate(parts, axis=1)


    lbl = lbl_ref[...]
    le = jnp.exp(lbl - jnp.max(lbl, axis=0, keepdims=True))
    lb = le[0:1, :] / jnp.sum(le, axis=0, keepdims=True)
    f_raw = [proj_chunk(4, c) for c in range(n_chunks)]
    ti = lax.broadcasted_iota(jnp.int32, (TM, TM), 0)
    si = lax.broadcasted_iota(jnp.int32, (TM, TM), 1)
    tri = jnp.where(ti >= si, 1.0, 0.0).astype(BF16)
    forget_c, b_c, qh_c = [], [], []
    for c in range(n_chunks):
        q_raw = proj_chunk(3, c)
        fg = lb[:, cols(c)] + (1.0 - lb[:, cols(c)]) * _sigmoid(f_raw[c])
        lf = jnp.log(fg)
        lf_hi = lf.astype(BF16)
        lf_lo = (lf - lf_hi.astype(F32)).astype(BF16)
        bc = jnp.dot(tri, lf_hi, preferred_element_type=F32) + jnp.dot(tri, lf_lo, preferred_element_type=F32)
        forget_c.append(fg)
        b_c.append(bc * LOG2E)
        qh_c.append(_silu(q_raw) * (HEAD_DIM ** -0.5))
    forget = row_cat(forget_c)
    kk = 1.0 - forget
    b = row_cat(b_c)
    qh = row_cat(qh_c)

    gv_c = [_gelu(proj_chunk(1, c)) for c in range(n_chunks)]
    vf = row_cat([proj_chunk(5, c) for c in range(n_chunks)])
    vv = vf.astype(BF16)
    za_early = [proj_chunk(2, 0), proj_chunk(2, 1)]
    mu = sum(jnp.sum(g, axis=-1, keepdims=True) for g in gv_c) * (1.0 / D_MODEL)
    vc_c = [g - mu for g in gv_c]
    var = sum(jnp.sum(v * v, axis=-1, keepdims=True) for v in vc_c) * (1.0 / D_MODEL)
    rstd = lax.rsqrt(var + EPS)

    r128 = lax.broadcasted_iota(jnp.int32, (GMLP_CHUNK, GMLP_CHUNK), 0)
    c128 = lax.broadcasted_iota(jnp.int32, (GMLP_CHUNK, GMLP_CHUNK), 1)
    gu_c, sp_c = [], []
    for g in range(GMLP_GROUPS):
        gu_c.append(_gelu(proj_chunk(0, g)))
        vn = (vc_c[g] * rstd * lng_ref[:, cols(g)] + lnb_ref[:, cols(g)]).astype(BF16)
        wg = jnp.where(r128 >= c128, wsp_ref[g], jnp.zeros((), BF16))
        bias = jnp.broadcast_to(bs_ref[:, g:g + 1], (GMLP_CHUNK, GMLP_GROUP_DIM))
        sp_c.append(jnp.concatenate(
            [jnp.dot(wg, vn[r * GMLP_CHUNK:(r + 1) * GMLP_CHUNK], preferred_element_type=F32) + bias
             for r in range(TM // GMLP_CHUNK)], axis=0))

    ya_parts = []

    def branch_a_chunk(c, za_raw):
        ya_in = (gu_c[c] * sp_c[c] * _silu(za_raw)).astype(BF16)
        w = pltpu.bitcast(wba_ref[c * chunk_cols // 2:(c + 1) * chunk_cols // 2, :], BF16)
        ya_parts.append(jnp.dot(ya_in, w, preferred_element_type=F32))

    za_raw = za_early
    branch_a_chunk(0, za_raw[0])
    za_raw.append(proj_chunk(2, 2))
    branch_a_chunk(1, za_raw[1])
    za_raw.append(proj_chunk(2, 3))
    branch_a_chunk(2, za_raw[2])

    def level_operand(hs):
        if hs >= SUBLANES:
            shp = (TM // (2 * hs), 2 * hs, D_MODEL)
            b3, q3, k3 = b.reshape(shp), qh.reshape(shp), kk.reshape(shp)
            r = b3[:, hs - 1:hs, :]
            xk = k3[:, :hs, :] * jnp.exp2(r - b3[:, :hs, :])
            xq = q3[:, hs:, :] * jnp.exp2(b3[:, hs:, :] - r)
            return xk, xq
        shp = (TM // SUBLANES, SUBLANES, D_MODEL)
        b8, q8, k8 = b.reshape(shp), qh.reshape(shp), kk.reshape(shp)
        row8 = lax.broadcasted_iota(jnp.int32, (1, SUBLANES, D_MODEL), 1)
        if hs == 4:
            r = jnp.broadcast_to(b8[:, 3:4, :], shp)
        else:
            r = jnp.where(row8 < 4, jnp.broadcast_to(b8[:, 1:2, :], shp), jnp.broadcast_to(b8[:, 5:6, :], shp))
        is_q = (row8 % (2 * hs)) >= hs
        d = (b8 - r) * jnp.where(is_q, 1.0, -1.0)
        x = jnp.where(is_q, q8, k8) * jnp.exp2(d)
        return x.reshape(TM, D_MODEL).astype(BF16)

    def as_rows(x3):
        return x3.reshape(TOP_HALF, D_MODEL).astype(BF16)

    xk_top, xq_top = (as_rows(x3) for x3 in level_operand(TOP_HALF))
    block_levels = []
    for hs in MXU_HALVES:
        if hs >= SUBLANES:
            xk, xq = level_operand(hs)
            x_all = jnp.concatenate([xk, xq], axis=1).reshape(TM, D_MODEL).astype(BF16)
            block_levels.append((hs, x_all, as_rows(xq)))
    row_levels = [(hs, level_operand(hs)) for hs in MXU_HALVES if hs < SUBLANES]

    b_last = b[TM - 1:TM, :]
    qs = (qh * jnp.exp2(b)).astype(BF16)
    ks = (kk * jnp.exp2(b_last - b)).astype(BF16)
    dl = jnp.exp2(b_last)

    rows = lax.broadcasted_iota(jnp.int32, (TM, D_MODEL), 0)
    w0 = qh * kk
    w1 = jnp.where((rows & 1) == 1, qh * pltpu.roll(kk, 1, 0) * forget, 0.0)

    lane = lax.broadcasted_iota(jnp.int32, (SUBLANES, LANES), 1)
    srow = lax.broadcasted_iota(jnp.int32, (SUBLANES, LANES), 0)
    srow3 = lax.broadcasted_iota(jnp.int32, (1, SUBLANES, HEAD_DIM), 1)
    mask_cache = {}

    def lane_range(off, n):
        key = (off, n)
        if key not in mask_cache:
            mask_cache[key] = (lane >= off) & (lane < off + n)
        return mask_cache[key]

    def diag_block(hs, off):
        key = (hs, off, 'd')
        if key not in mask_cache:
            c = lane - off
            mask_cache[key] = ((c >= 0) & (c < SUBLANES) & ((c ^ srow) < 2 * hs)
                               & ((srow & hs) != 0) & ((c & hs) == 0))
        return mask_cache[key]

    def diagonal(off, below):
        key = (off, below, 'g')
        if key not in mask_cache:
            m = lane == (srow + (off - below))
            mask_cache[key] = m & ((srow & 1) == 1) if below else m
        return mask_cache[key]

    pending = [('a', n_chunks - 1)] + [(j, c) for j in (6, 7, 8) for c in range(n_chunks)]
    chunks = {6: [], 7: [], 8: []}

    def filler():
        if pending:
            j, c = pending.pop(0)
            if j == 'a':
                branch_a_chunk(c, za_raw[c])
            else:
                chunks[j].append(proj_chunk(j, c))

    n_reg = TM // SUBLANES
    zero = jnp.zeros((SUBLANES, LANES), F32)
    hng = hng_ref[...]
    o_heads = []
    inter_pair = []

    def head_scores(hd):
        sl = slice(hd * HEAD_DIM, (hd + 1) * HEAD_DIM)
        a_t = [[zero] * n_reg for _ in range(TM // LANES)]
        p_top = _dot_nt(xq_top[:, sl], xk_top[:, sl])
        for i in range(TOP_HALF // SUBLANES):
            a_t[0][TOP_HALF // SUBLANES + i] = p_top[SUBLANES * i:SUBLANES * (i + 1)]
        for hs, x_all, xqc in block_levels:
            pc = _dot_nt(xqc[:, sl], x_all[:, sl])
            for j in range(TM // (2 * hs)):
                half, off = divmod(2 * hs * j, LANES)
                m = lane_range(off, hs)
                for kp in range(hs // SUBLANES):
                    src = j * hs + SUBLANES * kp
                    idx = (2 * hs * j + hs) // SUBLANES + kp
                    piece = pc[src:src + SUBLANES, half * LANES:(half + 1) * LANES]
                    a_t[half][idx] = jnp.where(m, piece, a_t[half][idx])
        filler()
        for hs, xl in row_levels:
            xh = xl[:, sl]
            if hs == 4:
                x3 = xh.astype(F32).reshape(TM // (2 * SUBLANES), 2, SUBLANES, HEAD_DIM)
                lo = pltpu.roll(x3[:, 0], 4, 1)
                xq4 = jnp.where(srow3 < 4, lo, x3[:, 1]).reshape(TM // 2, HEAD_DIM).astype(BF16)
                pq = _dot_nt(xq4, xh)
                for i in range(n_reg):
                    half, off = divmod(SUBLANES * i, LANES)
                    k = i // 2
                    piece = pq[SUBLANES * k:SUBLANES * (k + 1), half * LANES:(half + 1) * LANES]
                    if i % 2 == 0:
                        piece = pltpu.roll(piece, 4, 0)
                    a_t[half][i] = jnp.where(diag_block(hs, off), piece, a_t[half][i])
                continue
            x3 = xh.astype(F32).reshape(TM // (2 * SUBLANES), 2, SUBLANES, HEAD_DIM)
            ra, rb = x3[:, 0], x3[:, 1]
            lo = jnp.where(srow3 < 2, pltpu.roll(ra, 6, 1), pltpu.roll(ra, 4, 1))
            hi = jnp.where(srow3 < 6, pltpu.roll(rb, 2, 1), rb)
            xq2 = jnp.where(srow3 < 4, lo, hi).reshape(TM // 2, HEAD_DIM).astype(BF16)
            pq = _dot_nt(xq2, xh)
            for i in range(n_reg):
                half, off = divmod(SUBLANES * i, LANES)
                k = i // 2
                src = pq[SUBLANES * k:SUBLANES * (k + 1), half * LANES:(half + 1) * LANES]
                if i % 2 == 0:
                    piece = jnp.where(srow < 4, pltpu.roll(src, 2, 0), pltpu.roll(src, 4, 0))
                else:
                    piece = jnp.where(srow < 4, pltpu.roll(src, 6, 0), src)
                a_t[half][i] = jnp.where(diag_block(hs, off), piece, a_t[half][i])
        s0 = jnp.sum(w0[:, sl], axis=-1, keepdims=True)
        s1 = jnp.sum(w1[:, sl], axis=-1, keepdims=True)
        for i in range(n_reg):
            half, off = divmod(SUBLANES * i, LANES)
            rs = slice(SUBLANES * i, SUBLANES * (i + 1))
            below = jnp.where(diagonal(off, 1), jnp.broadcast_to(s1[rs], (SUBLANES, LANES)), a_t[half][i])
            a_t[half][i] = jnp.where(diagonal(off, 0), jnp.broadcast_to(s0[rs], (SUBLANES, LANES)), below)
        a = jnp.concatenate([jnp.concatenate(col, axis=0) for col in a_t], axis=1)
        return a.astype(BF16)

    def head_output(hd, a):
        sl = slice(hd * HEAD_DIM, (hd + 1) * HEAD_DIM)
        vh = vv[:, sl]
        st = st_ref[hd]
        if hd % 2 == 0:
            zz = jnp.zeros((HEAD_DIM, HEAD_DIM), BF16)
            st_pair = jnp.concatenate([
                jnp.concatenate([st.astype(BF16), zz], axis=1),
                jnp.concatenate([zz, st_ref[hd + 1].astype(BF16)], axis=1)], axis=0)
            pair_cols = slice(hd * HEAD_DIM, (hd + 2) * HEAD_DIM)
            inter_pair.append(_dot_nt(qs[:, pair_cols], st_pair))
        o_h = (jnp.dot(a, vh, preferred_element_type=F32)
               + inter_pair[-1][:, (hd % 2) * HEAD_DIM:(hd % 2 + 1) * HEAD_DIM])
        st_ref[hd] = st * dl[:, sl] + _dot_tn(vh, ks[:, sl])
        o_heads.append(_rms(o_h, hng[:, sl]))
        if hd % 2:
            filler()

    a_next = head_scores(0)
    for hd in range(HEADS):
        a_cur = a_next
        if hd + 1 < HEADS:
            a_next = head_scores(hd + 1)
        head_output(hd, a_cur)
    o_all = jnp.concatenate(o_heads, axis=1)
    while pending:
        filler()
    szb = _silu(row_cat(chunks[6]))
    y_a = ya_parts[0] + ya_parts[1] + ya_parts[2] + ya_parts[3]
    yb_in = (o_all * szb).astype(BF16)

    halves = [slice(i * TOP_HALF, (i + 1) * TOP_HALF) for i in range(2)]
    y_b = [jnp.dot(yb_in[r], _unpack(wbb_ref), preferred_element_type=F32) for r in halves]
    ga = _sigmoid(jnp.concatenate(chunks[7], axis=1))
    gb = _sigmoid(jnp.concatenate(chunks[8], axis=1))
    mo = [jnp.dot((ga[r] * y_a[r] + gb[r] * y_b[i]).astype(BF16), _unpack(wout_ref), preferred_element_type=F32)
          for i, r in enumerate(halves)]
    x1 = [x[r] + _rms(mo[i], gpost_ref[...]) for i, r in enumerate(halves)]
    gl = [jnp.dot(x1[i].astype(BF16), _unpack(wpg_ref), preferred_element_type=F32) + bpg_ref[...] for i in range(2)]
    for i, r in enumerate(halves):
        o_ref[r0 + r.start:r0 + r.stop, :] = x1[i] + _rms(e_ple[r] * _sigmoid(gl[i]), gple_ref[...])


def _pack_body(w_ref, o_ref):
    o_ref[...] = pltpu.bitcast(w_ref[...].astype(BF16), jnp.uint32)


def _pack_rows(w):
    k, n = w.shape
    bk, bn = min(k, PACK_BLOCK_ROWS), min(n, PACK_BLOCK_COLS)
    assert k % bk == 0 and n % bn == 0
    return pl.pallas_call(
        _pack_body,
        out_shape=jax.ShapeDtypeStruct((k // 2, n), jnp.uint32),
        grid=(k // bk, n // bn),
        in_specs=[pl.BlockSpec((bk, bn), lambda i, j: (i, j))],
        out_specs=pl.BlockSpec((bk // 2, bn), lambda i, j: (i, j)),
        name="pack_weight_rows",
    )(w)


def _const_spec(shape):
    nd = len(shape)
    return pl.BlockSpec(shape, lambda b, t: (0,) * nd, pipeline_mode=pl.Buffered(1))


@jax.jit
def kernel(x, p, w_in, gmlp_ln_g, gmlp_ln_b, gmlp_w_s, gmlp_b_s, hgrn_lb_logits, hgrn_norm_g, w_branch_a, w_branch_b, w_out, g_pre, g_post, w_ple, w_ple_gate, b_ple_gate, g_ple):
    bsz, t, d = x.shape
    assert d == D_MODEL and t % (TM * TILES_PER_STEP) == 0 and p.shape[0] == 1
    row = lambda a: a.reshape(1, -1).astype(F32)

    consts = (
        _pack_rows(w_in[0]),
        row(gmlp_ln_g[0]), row(gmlp_ln_b[0]),
        gmlp_w_s[0].astype(BF16),
        gmlp_b_s[0].T.astype(F32),
        hgrn_lb_logits.astype(F32),
        row(hgrn_norm_g[0]),
        _pack_rows(w_branch_a[0]), _pack_rows(w_branch_b[0]), _pack_rows(w_out[0]),
        row(g_pre[0]), row(g_post[0]),
        _pack_rows(w_ple[0]), _pack_rows(w_ple_gate[0]),
        row(b_ple_gate[0]), row(g_ple[0]),
    )
    tile = lambda width: pl.BlockSpec((None, TM * TILES_PER_STEP, width), lambda b, t: (b, t, 0))
    return pl.pallas_call(
        _body,
        out_shape=jax.ShapeDtypeStruct(x.shape, x.dtype),
        grid=(bsz, t // (TM * TILES_PER_STEP)),
        in_specs=[tile(D_MODEL), tile(PLE_DIM)] + [_const_spec(c.shape) for c in consts],
        out_specs=tile(D_MODEL),
        scratch_shapes=[pltpu.VMEM((HEADS, HEAD_DIM, HEAD_DIM), F32)],
        compiler_params=pltpu.CompilerParams(
            dimension_semantics=("arbitrary", "arbitrary"),
            vmem_limit_bytes=VMEM_LIMIT_BYTES,
        ),
        name="hybrid_block",
    )(x, p[0], *consts)
```
